```python
import math
import jax, jax.numpy as jnp
from jax import lax
import numpy as np

D_MODEL = 2048
BATCH = 1
SEQ = 16384
DEPTH = 4

CHUNK = 64
D_ATTN = D_MODEL // 2
D_SSM = D_MODEL - D_ATTN
ATTN_HEAD_DIM = 64
ATTN_VDIM = 2 * ATTN_HEAD_DIM
N_ATTN_HEADS = D_ATTN // ATTN_VDIM
Q_BLOCK = 128
SSM_HEAD_DIM = 64
N_SSM_HEADS = D_SSM // SSM_HEAD_DIM
SSM_STATE = 128
SSM_GROUPS = 4
CONV_WIDTH = 4
SSD_CHUNK = 128
D_XBC = D_SSM + 2 * SSM_GROUPS * SSM_STATE
D_FF = ((8 * D_MODEL // 3 + 255) // 256) * 256
D_IN_PROJ = 3 * D_ATTN + D_SSM + D_XBC + N_SSM_HEADS
EPS = 1e-5

kernel_name = "hybrid_diffattn_ssd_swiglu_trunk"


def rmsnorm(x, w):
    xf = x.astype(jnp.float32)
    y = xf * lax.rsqrt(jnp.mean(xf * xf, axis=-1, keepdims=True) + EPS)
    return (y * w.astype(jnp.float32)).astype(x.dtype)


def alibi_slopes(n_heads):
    start = 2.0 ** (-8.0 / n_heads)
    return start ** jnp.arange(1, n_heads + 1, dtype=jnp.float32)


def diff_attention(q, k, v, lam, lam_init, subln_w):
    b, s_len = q.shape[0], q.shape[1]
    n_blk = s_len // Q_BLOCK
    scale = ATTN_HEAD_DIM ** -0.5
    slopes = alibi_slopes(N_ATTN_HEADS)[None, :, None, None, None]
    k_pos = jnp.arange(s_len)
    q_blocks = q.reshape(b, n_blk, Q_BLOCK, N_ATTN_HEADS, 2, ATTN_HEAD_DIM).transpose(1, 0, 2, 3, 4, 5)

    def one_block(args):
        q_i, blk = args
        q_pos = blk * Q_BLOCK + jnp.arange(Q_BLOCK)
        scores = jnp.einsum('bqhjd,bkhjd->bhjqk', q_i, k,
                            preferred_element_type=jnp.float32) * scale
        dist = jnp.abs(q_pos[:, None] - k_pos[None, :]).astype(jnp.float32)
        allowed = (k_pos[None, :] // CHUNK) <= (q_pos[:, None] // CHUNK)
        scores = jnp.where(allowed, scores - slopes * dist, -jnp.inf)
        p = jax.nn.softmax(scores, axis=-1)
        a = p[:, :, 0] - lam * p[:, :, 1]
        return jnp.einsum('bhqk,bkhe->bqhe', a.astype(v.dtype), v)

    out = lax.map(one_block, (q_blocks, jnp.arange(n_blk)))
    out = out.transpose(1, 0, 2, 3, 4).reshape(b, s_len, N_ATTN_HEADS, ATTN_VDIM)
    out = rmsnorm(out, subln_w) * (1.0 - lam_init)
    return out.reshape(b, s_len, D_ATTN)


def causal_conv(u, w, bias):
    c = u.shape[-1]
    out = lax.conv_general_dilated(
        u, w.astype(u.dtype)[:, None, :], window_strides=(1,),
        padding=[(CONV_WIDTH - 1, 0)], dimension_numbers=('NWC', 'WIO', 'NWC'),
        feature_group_count=c)
    return out + bias


def ssd_scan(x, dt, a_head, b_mat, c_mat):
    b, s_len = x.shape[0], x.shape[1]
    nc, L, G, R = s_len // SSD_CHUNK, SSD_CHUNK, SSM_GROUPS, N_SSM_HEADS // SSM_GROUPS
    xd = (x.astype(jnp.float32) * dt[..., None]).reshape(b, nc, L, G, R, SSM_HEAD_DIM)
    a = (a_head * dt).reshape(b, nc, L, G, R).transpose(0, 3, 4, 1, 2)
    a_cs = jnp.cumsum(a, axis=-1)
    bc = b_mat.reshape(b, nc, L, G, SSM_STATE)
    cc = c_mat.reshape(b, nc, L, G, SSM_STATE)
    causal = jnp.tril(jnp.ones((L, L), dtype=bool))
    seg = a_cs[..., :, None] - a_cs[..., None, :]
    decay = jnp.exp(jnp.where(causal, seg, -jnp.inf))
    cb = jnp.einsum('bclgn,bcsgn->bgcls', cc, bc)
    y_diag = jnp.einsum('bgrcls,bcsgrp->bclgrp', cb[:, :, None] * decay, xd)
    decay_states = jnp.exp(a_cs[..., -1:] - a_cs)
    states = jnp.einsum('bcsgn,bgrcs,bcsgrp->bcgrpn', bc, decay_states, xd)
    chunk_decay = jnp.exp(a_cs[..., -1])

    def step(h, inp):
        st, dec = inp
        return h * dec[..., None, None] + st, h

    h0 = jnp.zeros((b, G, R, SSM_HEAD_DIM, SSM_STATE), jnp.float32)
    _, prev = lax.scan(step, h0, (states.transpose(1, 0, 2, 3, 4, 5),
                                  chunk_decay.transpose(3, 0, 1, 2)))
    prev = prev.transpose(1, 0, 2, 3, 4, 5)
    y_off = jnp.einsum('bclgn,bcgrpn,bgrcl->bclgrp', cc, prev, jnp.exp(a_cs))
    y = (y_diag + y_off).reshape(b, s_len, N_SSM_HEADS, SSM_HEAD_DIM)
    return y.astype(x.dtype)


def ssd_mixer(z, xbc, dt_raw, conv_w, conv_b, dt_bias, a_log, d_skip, norm_w):
    b, s_len = z.shape[0], z.shape[1]
    xbc = jax.nn.silu(causal_conv(xbc, conv_w, conv_b))
    xs = xbc[..., :D_SSM].reshape(b, s_len, N_SSM_HEADS, SSM_HEAD_DIM)
    b_mat = xbc[..., D_SSM:D_SSM + SSM_GROUPS * SSM_STATE].reshape(b, s_len, SSM_GROUPS, SSM_STATE)
    c_mat = xbc[..., D_SSM + SSM_GROUPS * SSM_STATE:].reshape(b, s_len, SSM_GROUPS, SSM_STATE)
    dt = jax.nn.softplus(dt_raw.astype(jnp.float32) + dt_bias.astype(jnp.float32))
    a_head = -jnp.exp(a_log.astype(jnp.float32))
    y = ssd_scan(xs, dt, a_head, b_mat, c_mat) + xs * d_skip[:, None]
    y = y.reshape(b, s_len, D_SSM)
    return rmsnorm(y * jax.nn.silu(z), norm_w)


def setup_inputs(seed: int = 0) -> dict:
    key = jax.random.key(seed)
    ks = jax.random.split(key, 20)
    f32 = jnp.float32
    nrm = lambda k, shape, s: jax.random.normal(k, shape, f32) * s
    dt0 = jnp.exp(jax.random.uniform(ks[8], (DEPTH, N_SSM_HEADS), f32,
                                     math.log(1e-3), math.log(1e-1)))
    return {
        "x": nrm(ks[0], (BATCH, SEQ, D_MODEL), 1.0),
        "norm_mix_w": 1.0 + nrm(ks[1], (DEPTH, D_MODEL), 0.01),
        "w_in": nrm(ks[2], (DEPTH, D_MODEL, D_IN_PROJ), D_MODEL ** -0.5),
        "diff_lambda": nrm(ks[3], (DEPTH, 4, ATTN_HEAD_DIM), 0.1),
        "subln_w": 1.0 + nrm(ks[4], (DEPTH, ATTN_VDIM), 0.01),
        "conv_w": nrm(ks[5], (DEPTH, CONV_WIDTH, D_XBC), CONV_WIDTH ** -0.5),
        "conv_b": nrm(ks[6], (DEPTH, D_XBC), 0.01),
        "dt_bias": dt0 + jnp.log(-jnp.expm1(-dt0)),
        "a_log": jnp.log(jax.random.uniform(ks[9], (DEPTH, N_SSM_HEADS), f32, 1.0, 16.0)),
        "d_skip": 1.0 + nrm(ks[10], (DEPTH, N_SSM_HEADS), 0.01),
        "ssm_norm_w": 1.0 + nrm(ks[11], (DEPTH, D_SSM), 0.01),
        "w_out": nrm(ks[12], (DEPTH, D_ATTN + D_SSM, D_MODEL), (D_ATTN + D_SSM) ** -0.5),
        "norm_ffn_w": 1.0 + nrm(ks[13], (DEPTH, D_MODEL), 0.01),
        "w_gate_up": nrm(ks[14], (DEPTH, D_MODEL, 2 * D_FF), D_MODEL ** -0.5),
        "w_down": nrm(ks[15], (DEPTH, D_FF, D_MODEL), D_FF ** -0.5),
        "norm_final_w": 1.0 + nrm(ks[16], (D_MODEL,), 0.01),
    }


def reference(x, norm_mix_w, w_in, diff_lambda, subln_w, conv_w, conv_b, dt_bias,
              a_log, d_skip, ssm_norm_w, w_out, norm_ffn_w, w_gate_up, w_down,
              norm_final_w):
    b, s_len = x.shape[0], x.shape[1]
    o_k = D_ATTN
    o_v = 2 * D_ATTN
    o_z = 3 * D_ATTN
    o_xbc = o_z + D_SSM
    o_dt = o_xbc + D_XBC
    for l in range(DEPTH):
        h = rmsnorm(x, norm_mix_w[l])
        proj = h @ w_in[l]
        q = proj[..., :o_k].reshape(b, s_len, N_ATTN_HEADS, 2, ATTN_HEAD_DIM)
        k = proj[..., o_k:o_v].reshape(b, s_len, N_ATTN_HEADS, 2, ATTN_HEAD_DIM)
        v = proj[..., o_v:o_z].reshape(b, s_len, N_ATTN_HEADS, ATTN_VDIM)
        z = proj[..., o_z:o_xbc]
        xbc = proj[..., o_xbc:o_dt]
        dt_raw = proj[..., o_dt:]
        lam_init = 0.8 - 0.6 * math.exp(-0.3 * l)
        lq1, lk1, lq2, lk2 = (diff_lambda[l, i].astype(jnp.float32) for i in range(4))
        lam = jnp.exp(jnp.dot(lq1, lk1)) - jnp.exp(jnp.dot(lq2, lk2)) + lam_init
        attn_out = diff_attention(q, k, v, lam, lam_init, subln_w[l])
        ssm_out = ssd_mixer(z, xbc, dt_raw, conv_w[l], conv_b[l], dt_bias[l],
                            a_log[l], d_skip[l], ssm_norm_w[l])
        mixed = jnp.concatenate([attn_out, ssm_out.astype(attn_out.dtype)], axis=-1)
        x = x + (mixed @ w_out[l]).astype(x.dtype)
        h = rmsnorm(x, norm_ffn_w[l])
        gu = h @ w_gate_up[l]
        x = x + ((jax.nn.silu(gu[..., :D_FF]) * gu[..., D_FF:]) @ w_down[l]).astype(x.dtype)
    return rmsnorm(x, norm_final_w)
```

```python
import functools
import math

import jax
import jax.numpy as jnp
from jax import lax
from jax.experimental import pallas as pl
from jax.experimental.pallas import tpu as pltpu

D_MODEL = 2048
CHUNK = 64
D_ATTN = D_MODEL // 2
D_SSM = D_MODEL - D_ATTN
ATTN_HEAD_DIM = 64
ATTN_VDIM = 2 * ATTN_HEAD_DIM
N_ATTN_HEADS = D_ATTN // ATTN_VDIM
SSM_HEAD_DIM = 64
N_SSM_HEADS = D_SSM // SSM_HEAD_DIM
SSM_STATE = 128
SSM_GROUPS = 4
CONV_WIDTH = 4
SSD_CHUNK = 128
D_XBC = D_SSM + 2 * SSM_GROUPS * SSM_STATE
D_FF = ((8 * D_MODEL // 3 + 255) // 256) * 256
EPS = 1e-5

LANES = 128
LOG2E = math.log2(math.e)
NEG_BIG = -1e30
VMEM_LIMIT = 56 * 1024 * 1024

F32 = jnp.float32
BF16 = jnp.bfloat16


def _params(*sem):
    return pltpu.CompilerParams(dimension_semantics=sem, vmem_limit_bytes=VMEM_LIMIT)


def _rms_matmul_kernel(x_ref, nw_ref, w_ref, cs_ref, o_ref, h_ref):
    @pl.when(pl.program_id(1) == 0)
    def _():
        xf = x_ref[...]
        ms = jnp.mean(xf * xf, axis=-1, keepdims=True)
        h_ref[...] = (xf * lax.rsqrt(ms + EPS) * nw_ref[...]).astype(BF16)

    acc = jnp.dot(h_ref[...], w_ref[...], preferred_element_type=F32)
    o_ref[...] = (acc * cs_ref[...]).astype(o_ref.dtype)


def _rms_matmul(x, nw, w, colscale, out_dtype, tm, tn):
    s, d = x.shape
    n = w.shape[1]
    return pl.pallas_call(
        _rms_matmul_kernel,
        grid=(s // tm, n // tn),
        in_specs=[
            pl.BlockSpec((tm, d), lambda i, j: (i, 0)),
            pl.BlockSpec((1, d), lambda i, j: (0, 0)),
            pl.BlockSpec((d, tn), lambda i, j: (0, j)),
            pl.BlockSpec((1, tn), lambda i, j: (0, j)),
        ],
        out_specs=pl.BlockSpec((tm, tn), lambda i, j: (i, j)),
        out_shape=jax.ShapeDtypeStruct((s, n), out_dtype),
        scratch_shapes=[pltpu.VMEM((tm, d), BF16)],
        compiler_params=_params("parallel", "arbitrary"),
        name="rms_matmul",
    )(x, nw, w, colscale)


def _attn_kernel(slopes_ref, dl_ref, q_ref, k_ref, v_ref, sw_ref, o_ref,
                 acc1_ref, acc2_ref, *, tq, lam_init):
    h = pl.program_id(0)
    i = pl.program_id(1)
    c = slopes_ref[h] * LOG2E

    q = q_ref[...]
    lane = lax.broadcasted_iota(jnp.int32, q.shape, 1)
    zero = jnp.zeros_like(q)
    q_maps = (jnp.where(lane < ATTN_HEAD_DIM, q, zero),
              jnp.where(lane >= ATTN_HEAD_DIM, q, zero))

    kl = lax.broadcasted_iota(jnp.int32, (tq, tq), 0)
    ql = lax.broadcasted_iota(jnp.int32, (tq, tq), 1)
    rel = (ql - kl).astype(F32)
    bias_past = -c * rel
    allowed = (kl // CHUNK) <= (ql // CHUNK)
    bias_diag = jnp.where(allowed, -c * jnp.abs(rel), NEG_BIG)

    acc_refs = (acc1_ref, acc2_ref)
    acc1_ref[...] = jnp.zeros_like(acc1_ref)
    acc2_ref[...] = jnp.zeros_like(acc2_ref)

    def block(k0, bias, off, stats):
        kblk = k_ref[pl.ds(k0, tq), :]
        vblk = v_ref[pl.ds(k0, tq), :]
        new_stats = []
        for j in range(2):
            m, l = stats[j]
            s = lax.dot_general(kblk, q_maps[j], (((1,), (1,)), ((), ())),
                                preferred_element_type=F32) + bias
            m_new = jnp.maximum(m, jnp.max(s, axis=0, keepdims=True) + off)
            alpha = jnp.exp2(m - m_new)
            p = jnp.exp2(s - (m_new - off))
            l = alpha * l + jnp.sum(p, axis=0, keepdims=True)
            pv = lax.dot_general(vblk, p.astype(BF16), (((0,), (0,)), ((), ())),
                                 preferred_element_type=F32)
            acc_refs[j][...] = alpha * acc_refs[j][...] + pv
            new_stats.append((m_new, l))
        return tuple(new_stats)

    q0 = i * tq
    init = jnp.full((1, tq), NEG_BIG, F32), jnp.zeros((1, tq), F32)

    def past_block(kb, stats):
        k0 = pl.multiple_of(kb * tq, tq)
        off = -c * (q0 - k0).astype(F32)
        return block(k0, bias_past, off, stats)

    stats = lax.fori_loop(0, i, past_block, (init, init))
    stats = block(pl.multiple_of(q0, tq), bias_diag, 0.0, stats)

    dl = dl_ref[...]
    lam = (jnp.exp(jnp.sum(dl[0:1] * dl[1:2], axis=-1, keepdims=True))
           - jnp.exp(jnp.sum(dl[2:3] * dl[3:4], axis=-1, keepdims=True)) + lam_init)
    o = (acc1_ref[...] * (1.0 / stats[0][1])
         - lam * (acc2_ref[...] * (1.0 / stats[1][1])))
    ms = jnp.mean(o * o, axis=0, keepdims=True)
    y = (o * lax.rsqrt(ms + EPS) * sw_ref[...]) * (1.0 - lam_init)
    o_ref[...] = y.T.astype(o_ref.dtype)


def _diff_attention(proj, slopes, dl, subw_col, lam_init, tq):
    s = proj.shape[0]
    kern = functools.partial(_attn_kernel, tq=tq, lam_init=lam_init)
    return pl.pallas_call(
        kern,
        grid=(N_ATTN_HEADS, s // tq),
        in_specs=[
            pl.BlockSpec(memory_space=pltpu.SMEM),
            pl.BlockSpec((4, ATTN_HEAD_DIM), lambda h, i: (0, 0)),
            pl.BlockSpec((tq, LANES), lambda h, i: (i, h)),
            pl.BlockSpec((s, LANES), lambda h, i: (0, N_ATTN_HEADS + h)),
            pl.BlockSpec((s, LANES), lambda h, i: (0, 2 * N_ATTN_HEADS + h)),
            pl.BlockSpec((ATTN_VDIM, 1), lambda h, i: (0, 0)),
        ],
        out_specs=pl.BlockSpec((tq, LANES), lambda h, i: (i, h)),
        out_shape=jax.ShapeDtypeStruct((s, D_ATTN), BF16),
        scratch_shapes=[pltpu.VMEM((ATTN_VDIM, tq), F32),
                        pltpu.VMEM((ATTN_VDIM, tq), F32)],
        compiler_params=_params("parallel", "arbitrary"),
        name="diff_attention",
    )(slopes, dl, proj, proj, proj, subw_col)


def _split3(a):
    hi = a.astype(BF16)
    r1 = a - hi.astype(F32)
    mid = r1.astype(BF16)
    lo = (r1 - mid.astype(F32)).astype(BF16)
    return hi, mid, lo


def _dot3(a, b_bf16):
    out = None
    for t in _split3(a):
        d = jnp.dot(t, b_bf16, preferred_element_type=F32)
        out = d if out is None else out + d
    return out


def _silu(x):
    return x * (1.0 / (1.0 + jnp.exp(-x)))


def _ssd_kernel(z_ref, xr_ref, bcr_ref, dtr_ref, cw_ref, cb_ref, dtb_ref, alog_ref,
                dskip_ref, nw_ref, o_ref, ext_ref, state_ref):
    L = SSD_CHUNK
    c = pl.program_id(0)

    @pl.when(c == 0)
    def _():
        ext_ref[0:8, :] = jnp.zeros((8, D_XBC), F32)
        state_ref[...] = jnp.zeros_like(state_ref)

    ext_ref[8:8 + L, 0:D_SSM] = xr_ref[...].astype(F32)
    ext_ref[8:8 + L, D_SSM:D_XBC] = bcr_ref[...].astype(F32)
    conv = cb_ref[...]
    for t in range(CONV_WIDTH):
        conv = conv + cw_ref[t:t + 1, :] * ext_ref[5 + t:5 + t + L, :]
    ext_ref[0:8, :] = ext_ref[L:L + 8, :]
    xbc = _silu(conv)
    xs = xbc[:, :D_SSM]

    dt = dtr_ref[...] + dtb_ref[...]
    dt = jnp.maximum(dt, 0.0) + jnp.log1p(jnp.exp(-jnp.abs(dt)))
    a = -jnp.exp(alog_ref[...]) * dt
    row = lax.broadcasted_iota(jnp.int32, (L, L), 0)
    col = lax.broadcasted_iota(jnp.int32, (L, L), 1)
    causal = col <= row
    tril = jnp.where(causal, 1.0, 0.0).astype(BF16)
    a_cs = None
    for t in _split3(a):
        d = jnp.dot(tril, t, preferred_element_type=F32)
        a_cs = d if a_cs is None else a_cs + d
    a_cs_t = a_cs.T
    a_end = a_cs[L - 1:L, :]

    hrow = lax.broadcasted_iota(jnp.int32, (LANES, D_SSM), 0)
    ccol = lax.broadcasted_iota(jnp.int32, (LANES, D_SSM), 1)
    expand = jnp.where(ccol // SSM_HEAD_DIM == hrow, 1.0, 0.0).astype(BF16)
    dt_x = _dot3(dt, expand)
    grow_x = _dot3(jnp.exp(a_cs), expand)
    tail_x = _dot3(jnp.exp(a_end - a_cs), expand)
    end_x = grow_x[L - 1:L, :]

    xd = xs * dt_x
    xd_b = xd.astype(BF16)
    xdt_b = (xd * tail_x).astype(BF16)
    lane = lax.broadcasted_iota(jnp.int32, (L, LANES), 1)
    lo_half = lane < SSM_HEAD_DIM
    zeros_b = jnp.zeros((L, LANES), BF16)

    y_parts = []
    for g in range(SSM_GROUPS):
        b_g = xbc[:, D_SSM + g * SSM_STATE:D_SSM + (g + 1) * SSM_STATE].astype(BF16)
        c_g = xbc[:, D_SSM + (SSM_GROUPS + g) * SSM_STATE:
                  D_SSM + (SSM_GROUPS + g + 1) * SSM_STATE].astype(BF16)
        cb = lax.dot_general(c_g, b_g, (((1,), (1,)), ((), ())),
                             preferred_element_type=F32)
        for pr in range(2):
            pair = 2 * g + pr
            lanes = slice(pair * LANES, (pair + 1) * LANES)
            xd_p = xd_b[:, lanes]
            y_p = None
            for hh in range(2):
                head = 2 * pair + hh
                seg = a_cs[:, head:head + 1] - a_cs_t[head:head + 1, :]
                decay = jnp.exp(jnp.where(causal, seg, NEG_BIG))
                w = (cb * decay).astype(BF16)
                xh = jnp.where(lo_half if hh == 0 else ~lo_half, xd_p, zeros_b)
                d = jnp.dot(w, xh, preferred_element_type=F32)
                y_p = d if y_p is None else y_p + d
            st = state_ref[pair]
            y_off = jnp.dot(c_g, st.astype(BF16), preferred_element_type=F32)
            y_parts.append(y_p + y_off * grow_x[:, lanes])
            new = lax.dot_general(b_g, xdt_b[:, lanes], (((0,), (0,)), ((), ())),
                                  preferred_element_type=F32)
            state_ref[pair] = st * end_x[:, lanes] + new

    y = jnp.concatenate(y_parts, axis=-1) + xs * dskip_ref[...]
    gated = y * _silu(z_ref[...].astype(F32))
    ms = jnp.mean(gated * gated, axis=-1, keepdims=True)
    o_ref[...] = (gated * lax.rsqrt(ms + EPS) * nw_ref[...]).astype(o_ref.dtype)


def _ssd_mixer(proj, dt_raw, conv_w, conv_b, dt_bias, a_log, d_skip_x, norm_w):
    s = proj.shape[0]
    L = SSD_CHUNK
    full = lambda shape: pl.BlockSpec(shape, lambda c: (0, 0))
    return pl.pallas_call(
        _ssd_kernel,
        grid=(s // L,),
        in_specs=[
            pl.BlockSpec((L, D_SSM), lambda c: (c, 3)),
            pl.BlockSpec((L, D_SSM), lambda c: (c, 4)),
            pl.BlockSpec((L, D_SSM), lambda c: (c, 5)),
            pl.BlockSpec((L, LANES), lambda c: (c, 0)),
            full((CONV_WIDTH, D_XBC)),
            full((1, D_XBC)),
            full((1, LANES)),
            full((1, LANES)),
            full((1, D_SSM)),
            full((1, D_SSM)),
        ],
        out_specs=pl.BlockSpec((L, D_SSM), lambda c: (c, 0)),
        out_shape=jax.ShapeDtypeStruct((s, D_SSM), BF16),
        scratch_shapes=[pltpu.VMEM((L + 8, D_XBC), F32),
                        pltpu.VMEM((N_SSM_HEADS // 2, SSM_STATE, LANES), F32)],
        compiler_params=_params("arbitrary"),
        name="ssd_mixer",
    )(proj, proj, proj, dt_raw, conv_w, conv_b, dt_bias, a_log, d_skip_x, norm_w)


def _out_proj_kernel(a_ref, b_ref, wa_ref, wb_ref, x_ref, o_ref):
    acc = jnp.dot(a_ref[...], wa_ref[...], preferred_element_type=F32)
    acc = acc + jnp.dot(b_ref[...], wb_ref[...], preferred_element_type=F32)
    o_ref[...] = x_ref[...] + acc


def _out_proj(a, b, w, x, tm, tn):
    s = x.shape[0]
    ka, kb = a.shape[1], b.shape[1]
    return pl.pallas_call(
        _out_proj_kernel,
        grid=(s // tm, D_MODEL // tn),
        in_specs=[
            pl.BlockSpec((tm, ka), lambda i, j: (i, 0)),
            pl.BlockSpec((tm, kb), lambda i, j: (i, 0)),
            pl.BlockSpec((ka, tn), lambda i, j: (0, j)),
            pl.BlockSpec((kb, tn), lambda i, j: (1, j)),
            pl.BlockSpec((tm, tn), lambda i, j: (i, j)),
        ],
        out_specs=pl.BlockSpec((tm, tn), lambda i, j: (i, j)),
        out_shape=jax.ShapeDtypeStruct((s, D_MODEL), F32),
        compiler_params=_params("parallel", "arbitrary"),
        name="out_proj",
    )(a, b, w, w, x)


def _ffn_kernel(x_ref, nw_ref, wg_ref, wu_ref, wd_ref, fw_ref, o_ref, h_ref, *, final_norm):
    f = pl.program_id(1)

    @pl.when(f == 0)
    def _():
        xf = x_ref[...]
        ms = jnp.mean(xf * xf, axis=-1, keepdims=True)
        h_ref[...] = (xf * lax.rsqrt(ms + EPS) * nw_ref[...]).astype(BF16)
        o_ref[...] = xf

    h = h_ref[...]
    g = jnp.dot(h, wg_ref[...], preferred_element_type=F32)
    u = jnp.dot(h, wu_ref[...], preferred_element_type=F32)
    act = (_silu(g) * u).astype(BF16)
    o_ref[...] += jnp.dot(act, wd_ref[...], preferred_element_type=F32)

    if final_norm:
        @pl.when(f == pl.num_programs(1) - 1)
        def _():
            xf = o_ref[...]
            ms = jnp.mean(xf * xf, axis=-1, keepdims=True)
            o_ref[...] = xf * lax.rsqrt(ms + EPS) * fw_ref[...]


def _ffn(x, nw, w_gu, w_d, final_w, final_norm, tm, tf):
    s, d = x.shape
    nf = D_FF // tf
    kern = functools.partial(_ffn_kernel, final_norm=final_norm)
    return pl.pallas_call(
        kern,
        grid=(s // tm, nf),
        in_specs=[
            pl.BlockSpec((tm, d), lambda i, f: (i, 0)),
            pl.BlockSpec((1, d), lambda i, f: (0, 0)),
            pl.BlockSpec((d, tf), lambda i, f: (0, f)),
            pl.BlockSpec((d, tf), lambda i, f: (0, nf + f)),
            pl.BlockSpec((tf, d), lambda i, f: (f, 0)),
            pl.BlockSpec((1, d), lambda i, f: (0, 0)),
        ],
        out_specs=pl.BlockSpec((tm, d), lambda i, f: (i, 0)),
        out_shape=jax.ShapeDtypeStruct((s, d), F32),
        scratch_shapes=[pltpu.VMEM((tm, d), BF16)],
        compiler_params=_params("parallel", "arbitrary"),
        name="ffn",
    )(x, nw, w_gu, w_gu, w_d, final_w)


def _alibi_slopes(n_heads):
    start = 2.0 ** (-8.0 / n_heads)
    return start ** jnp.arange(1, n_heads + 1, dtype=F32)


def kernel(x, norm_mix_w, w_in, diff_lambda, subln_w, conv_w, conv_b, dt_bias, a_log,
           d_skip, ssm_norm_w, w_out, norm_ffn_w, w_gate_up, w_down, norm_final_w):
    b, s_len, _ = x.shape
    assert b == 1
    depth = w_in.shape[0]
    n_qkvzx = 3 * D_ATTN + D_SSM + D_XBC
    n_dt = w_in.shape[2] - n_qkvzx
    tm = min(512, s_len)
    tq = min(256, s_len)

    slopes = _alibi_slopes(N_ATTN_HEADS)
    q_scale = ATTN_HEAD_DIM ** -0.5 * LOG2E
    colscale = jnp.concatenate([jnp.full((1, D_ATTN), q_scale, F32),
                                jnp.ones((1, n_qkvzx - D_ATTN), F32)], axis=1)
    ones_dt = jnp.ones((1, LANES), F32)
    pad_l = lambda v: jnp.pad(v.astype(F32), (0, LANES - v.shape[0]))[None, :]

    xc = x[0]
    for l in range(depth):
        w_main = w_in[l, :, :n_qkvzx].astype(BF16)
        w_dt = jnp.pad(w_in[l, :, n_qkvzx:], ((0, 0), (0, LANES - n_dt))).astype(BF16)
        nw = norm_mix_w[l][None, :]
        proj = _rms_matmul(xc, nw, w_main, colscale, BF16, tm, 1024)
        dt_raw = _rms_matmul(xc, nw, w_dt, ones_dt, F32, tm, LANES)

        lam_init = 0.8 - 0.6 * math.exp(-0.3 * l)
        attn_out = _diff_attention(proj, slopes, diff_lambda[l], subln_w[l][:, None],
                                   lam_init, tq)
        ssm_out = _ssd_mixer(proj, dt_raw, conv_w[l], conv_b[l][None, :], pad_l(dt_bias[l]),
                             pad_l(a_log[l]), jnp.repeat(d_skip[l], SSM_HEAD_DIM)[None, :],
                             ssm_norm_w[l][None, :])
        xc = _out_proj(attn_out, ssm_out, w_out[l].astype(BF16), xc, tm, 1024)
        xc = _ffn(xc, norm_ffn_w[l][None, :], w_gate_up[l].astype(BF16),
                  w_down[l].astype(BF16), norm_final_w[None, :], l == depth - 1, tm, 512)
    return xc[None]
```

```python
import functools
import math

import jax
import jax.numpy as jnp
from jax import lax
from jax.experimental import pallas as pl
from jax.experimental.pallas import tpu as pltpu

D_MODEL = 2048
CHUNK = 64
D_ATTN = D_MODEL // 2
D_SSM = D_MODEL - D_ATTN
ATTN_HEAD_DIM = 64
ATTN_VDIM = 2 * ATTN_HEAD_DIM
N_ATTN_HEADS = D_ATTN // ATTN_VDIM
SSM_HEAD_DIM = 64
N_SSM_HEADS = D_SSM // SSM_HEAD_DIM
SSM_STATE = 128
SSM_GROUPS = 4
CONV_WIDTH = 4
SSD_CHUNK = 128
D_XBC = D_SSM + 2 * SSM_GROUPS * SSM_STATE
D_FF = ((8 * D_MODEL // 3 + 255) // 256) * 256
EPS = 1e-5

LANES = 128
LOG2E = math.log2(math.e)
NEG_BIG = -1e30
VMEM_LIMIT = 56 * 1024 * 1024

F32 = jnp.float32
BF16 = jnp.bfloat16


def _params(*sem):
    return pltpu.CompilerParams(dimension_semantics=sem, vmem_limit_bytes=VMEM_LIMIT)


def _rms_matmul_kernel(x_ref, nw_ref, w_ref, cs_ref, o_ref, h_ref):
    @pl.when(pl.program_id(1) == 0)
    def _():
        xf = x_ref[...]
        ms = jnp.mean(xf * xf, axis=-1, keepdims=True)
        h_ref[...] = (xf * lax.rsqrt(ms + EPS) * nw_ref[...]).astype(BF16)

    acc = jnp.dot(h_ref[...], w_ref[...], preferred_element_type=F32)
    o_ref[...] = (acc * cs_ref[...]).astype(o_ref.dtype)


def _rms_matmul(x, nw, w, colscale, out_dtype, tm, tn):
    s, d = x.shape
    n = w.shape[1]
    return pl.pallas_call(
        _rms_matmul_kernel,
        grid=(s // tm, n // tn),
        in_specs=[
            pl.BlockSpec((tm, d), lambda i, j: (i, 0)),
            pl.BlockSpec((1, d), lambda i, j: (0, 0)),
            pl.BlockSpec((d, tn), lambda i, j: (0, j)),
            pl.BlockSpec((1, tn), lambda i, j: (0, j)),
        ],
        out_specs=pl.BlockSpec((tm, tn), lambda i, j: (i, j)),
        out_shape=jax.ShapeDtypeStruct((s, n), out_dtype),
        scratch_shapes=[pltpu.VMEM((tm, d), BF16)],
        compiler_params=_params("parallel", "arbitrary"),
        name="rms_matmul",
    )(x, nw, w, colscale)


def _split3(a):
    hi = a.astype(BF16)
    r1 = a - hi.astype(F32)
    mid = r1.astype(BF16)
    lo = (r1 - mid.astype(F32)).astype(BF16)
    return hi, mid, lo


def _attn_kernel(slopes_ref, dl_ref, q_ref, k_ref, v_ref, sw_ref, o_ref,
                 acc1_ref, acc2_ref, s_ref, p_ref, *, tq, lam_init):
    h = pl.program_id(0)
    i = pl.program_id(1)
    c = slopes_ref[h] * LOG2E

    q = q_ref[...]
    lane = lax.broadcasted_iota(jnp.int32, q.shape, 1)
    zero = jnp.zeros_like(q)
    c_terms = _split3(jnp.full(q.shape, c, F32))
    q_bias = jnp.zeros(q.shape, F32)
    for t in range(3):
        q_bias = jnp.where((lane >> 1) == t, c_terms[t].astype(F32), q_bias)
    q_bias = q_bias.astype(BF16)
    q_maps = (jnp.concatenate([jnp.where(lane < ATTN_HEAD_DIM, q, zero), q_bias], axis=1),
              jnp.concatenate([jnp.where(lane >= ATTN_HEAD_DIM, q, zero), q_bias], axis=1))
    krow = lax.broadcasted_iota(jnp.int32, q.shape, 0)
    k_split = jnp.where((lane & 1) == 0, (krow >> 4) << 4, krow & 15)
    k_bias = jnp.where(lane < 6, k_split, 0).astype(F32).astype(BF16)

    kl = lax.broadcasted_iota(jnp.int32, (tq, tq), 0)
    ql = lax.broadcasted_iota(jnp.int32, (tq, tq), 1)
    allowed = (kl // CHUNK) <= (ql // CHUNK)
    bias_diag = jnp.where(allowed, (-2.0 * c) * jnp.maximum(kl - ql, 0).astype(F32), NEG_BIG)

    acc_refs = (acc1_ref, acc2_ref)
    acc1_ref[...] = jnp.zeros_like(acc1_ref)
    acc2_ref[...] = jnp.zeros_like(acc2_ref)
    p_ref[1] = jnp.zeros(p_ref.shape[1:], BF16)

    def scores(blk, slot):
        k0 = pl.multiple_of(blk * tq, tq)
        kblk = jnp.concatenate([k_ref[pl.ds(k0, tq), :], k_bias], axis=1)
        for j in range(2):
            s_ref[slot, j] = lax.dot_general(kblk, q_maps[j], (((1,), (1,)), ((), ())),
                                             preferred_element_type=F32)

    def softmax(slot, bias, off, stats):
        new_stats, alphas = [], []
        for j in range(2):
            m, l = stats[j]
            s = s_ref[slot, j]
            if bias is not None:
                s = s + bias
            m_new = jnp.maximum(m, jnp.max(s, axis=0, keepdims=True) + off)
            alpha = jnp.exp2(m - m_new)
            p = jnp.exp2(s - (m_new - off))
            l = alpha * l + jnp.sum(p, axis=0, keepdims=True)
            p_ref[slot, j] = p.astype(BF16)
            new_stats.append((m_new, l))
            alphas.append(alpha)
        return tuple(new_stats), tuple(alphas)

    def values(blk, slot, alphas):
        k0 = pl.multiple_of(blk * tq, tq)
        vblk = v_ref[pl.ds(k0, tq), :]
        for j in range(2):
            pv = lax.dot_general(vblk, p_ref[slot, j], (((0,), (0,)), ((), ())),
                                 preferred_element_type=F32)
            acc_refs[j][...] = alphas[j] * acc_refs[j][...] + pv

    init = jnp.full((1, tq), NEG_BIG, F32), jnp.zeros((1, tq), F32)
    ones = jnp.ones((1, tq), F32)
    scores(0, 0)

    def past_block(t, slot, carry):
        stats, alphas = carry
        values(jnp.maximum(t - 1, 0), 1 - slot, alphas)
        off = -c * ((i - t) * tq).astype(F32)
        stats, alphas = softmax(slot, None, off, stats)
        scores(t + 1, 1 - slot)
        return stats, alphas

    def past_pair(u, carry):
        return past_block(2 * u + 1, 1, past_block(2 * u, 0, carry))

    carry = lax.fori_loop(0, i >> 1, past_pair, ((init, init), (ones, ones)))
    carry = lax.fori_loop(0, i & 1, lambda _, cr: past_block(i - 1, 0, cr), carry)

    dl = dl_ref[...]
    lam = (jnp.exp(jnp.sum(dl[0:1] * dl[1:2], axis=-1, keepdims=True))
           - jnp.exp(jnp.sum(dl[2:3] * dl[3:4], axis=-1, keepdims=True)) + lam_init)

    def finish(slot):
        stats, alphas = carry
        values(jnp.maximum(i - 1, 0), 1 - slot, alphas)
        stats, alphas = softmax(slot, bias_diag, 0.0, stats)
        values(i, slot, alphas)
        o = (acc1_ref[...] * (1.0 / stats[0][1])
             - lam * (acc2_ref[...] * (1.0 / stats[1][1])))
        ms = jnp.mean(o * o, axis=0, keepdims=True)
        y = (o * lax.rsqrt(ms + EPS) * sw_ref[...]) * (1.0 - lam_init)
        o_ref[...] = y.T.astype(o_ref.dtype)

    for parity in range(2):
        pl.when((i & 1) == parity)(functools.partial(finish, parity))


def _diff_attention(proj, slopes, dl, subw_col, lam_init, tq):
    s = proj.shape[0]
    kern = functools.partial(_attn_kernel, tq=tq, lam_init=lam_init)
    return pl.pallas_call(
        kern,
        grid=(N_ATTN_HEADS, s // tq),
        in_specs=[
            pl.BlockSpec(memory_space=pltpu.SMEM),
            pl.BlockSpec((4, ATTN_HEAD_DIM), lambda h, i: (0, 0)),
            pl.BlockSpec((tq, LANES), lambda h, i: (i, h)),
            pl.BlockSpec((s, LANES), lambda h, i: (0, N_ATTN_HEADS + h)),
            pl.BlockSpec((s, LANES), lambda h, i: (0, 2 * N_ATTN_HEADS + h)),
            pl.BlockSpec((ATTN_VDIM, 1), lambda h, i: (0, 0)),
        ],
        out_specs=pl.BlockSpec((tq, LANES), lambda h, i: (i, h)),
        out_shape=jax.ShapeDtypeStruct((s, D_ATTN), BF16),
        scratch_shapes=[pltpu.VMEM((ATTN_VDIM, tq), F32),
                        pltpu.VMEM((ATTN_VDIM, tq), F32),
                        pltpu.VMEM((2, 2, tq, tq), F32),
                        pltpu.VMEM((2, 2, tq, tq), BF16)],
        compiler_params=_params("parallel", "arbitrary"),
        name="diff_attention",
    )(slopes, dl, proj, proj, proj, subw_col)


def _dot3(a, b_bf16):
    out = None
    for t in _split3(a):
        d = jnp.dot(t, b_bf16, preferred_element_type=F32)
        out = d if out is None else out + d
    return out


def _silu(x):
    return x * (1.0 / (1.0 + jnp.exp(-x)))


def _ssd_kernel(z_ref, xr_ref, bcr_ref, dtr_ref, cw_ref, cb_ref, dtb_ref, alog_ref,
                dskip_ref, nw_ref, o_ref, ext_ref, state_ref):
    L = SSD_CHUNK
    c = pl.program_id(0)

    @pl.when(c == 0)
    def _():
        ext_ref[0:8, :] = jnp.zeros((8, D_XBC), F32)
        state_ref[...] = jnp.zeros_like(state_ref)

    ext_ref[8:8 + L, 0:D_SSM] = xr_ref[...].astype(F32)
    ext_ref[8:8 + L, D_SSM:D_XBC] = bcr_ref[...].astype(F32)
    conv = cb_ref[...]
    for t in range(CONV_WIDTH):
        conv = conv + cw_ref[t:t + 1, :] * ext_ref[5 + t:5 + t + L, :]
    ext_ref[0:8, :] = ext_ref[L:L + 8, :]
    xbc = _silu(conv)
    xs = xbc[:, :D_SSM]

    dt = dtr_ref[...] + dtb_ref[...]
    dt = jnp.maximum(dt, 0.0) + jnp.log1p(jnp.exp(-jnp.abs(dt)))
    a = -jnp.exp(alog_ref[...]) * dt
    row = lax.broadcasted_iota(jnp.int32, (L, L), 0)
    col = lax.broadcasted_iota(jnp.int32, (L, L), 1)
    causal = col <= row
    tril = jnp.where(causal, 1.0, 0.0).astype(BF16)
    a_cs = None
    for t in _split3(a):
        d = jnp.dot(tril, t, preferred_element_type=F32)
        a_cs = d if a_cs is None else a_cs + d
    a_cs_t = a_cs.T
    a_end = a_cs[L - 1:L, :]

    hrow = lax.broadcasted_iota(jnp.int32, (LANES, D_SSM), 0)
    ccol = lax.broadcasted_iota(jnp.int32, (LANES, D_SSM), 1)
    expand = jnp.where(ccol // SSM_HEAD_DIM == hrow, 1.0, 0.0).astype(BF16)
    dt_x = _dot3(dt, expand)
    grow_x = _dot3(jnp.exp(a_cs), expand)
    tail_x = _dot3(jnp.exp(a_end - a_cs), expand)
    end_x = grow_x[L - 1:L, :]

    xd = xs * dt_x
    xd_b = xd.astype(BF16)
    xdt_b = (xd * tail_x).astype(BF16)
    lane = lax.broadcasted_iota(jnp.int32, (L, LANES), 1)
    lo_half = lane < SSM_HEAD_DIM
    zeros_b = jnp.zeros((L, LANES), BF16)

    y_parts = []
    for g in range(SSM_GROUPS):
        b_g = xbc[:, D_SSM + g * SSM_STATE:D_SSM + (g + 1) * SSM_STATE].astype(BF16)
        c_g = xbc[:, D_SSM + (SSM_GROUPS + g) * SSM_STATE:
                  D_SSM + (SSM_GROUPS + g + 1) * SSM_STATE].astype(BF16)
        cb = lax.dot_general(c_g, b_g, (((1,), (1,)), ((), ())),
                             preferred_element_type=F32)
        for pr in range(2):
            pair = 2 * g + pr
            lanes = slice(pair * LANES, (pair + 1) * LANES)
            xd_p = xd_b[:, lanes]
            y_p = None
            for hh in range(2):
                head = 2 * pair + hh
                seg = a_cs[:, head:head + 1] - a_cs_t[head:head + 1, :]
                decay = jnp.exp(jnp.where(causal, seg, NEG_BIG))
                w = (cb * decay).astype(BF16)
                xh = jnp.where(lo_half if hh == 0 else ~lo_half, xd_p, zeros_b)
                d = jnp.dot(w, xh, preferred_element_type=F32)
                y_p = d if y_p is None else y_p + d
            st = state_ref[pair]
            y_off = jnp.dot(c_g, st.astype(BF16), preferred_element_type=F32)
            y_parts.append(y_p + y_off * grow_x[:, lanes])
            new = lax.dot_general(b_g, xdt_b[:, lanes], (((0,), (0,)), ((), ())),
                                  preferred_element_type=F32)
            state_ref[pair] = st * end_x[:, lanes] + new

    y = jnp.concatenate(y_parts, axis=-1) + xs * dskip_ref[...]
    gated = y * _silu(z_ref[...].astype(F32))
    ms = jnp.mean(gated * gated, axis=-1, keepdims=True)
    o_ref[...] = (gated * lax.rsqrt(ms + EPS) * nw_ref[...]).astype(o_ref.dtype)


def _ssd_mixer(proj, dt_raw, conv_w, conv_b, dt_bias, a_log, d_skip_x, norm_w):
    s = proj.shape[0]
    L = SSD_CHUNK
    full = lambda shape: pl.BlockSpec(shape, lambda c: (0, 0))
    return pl.pallas_call(
        _ssd_kernel,
        grid=(s // L,),
        in_specs=[
            pl.BlockSpec((L, D_SSM), lambda c: (c, 3)),
            pl.BlockSpec((L, D_SSM), lambda c: (c, 4)),
            pl.BlockSpec((L, D_SSM), lambda c: (c, 5)),
            pl.BlockSpec((L, LANES), lambda c: (c, 0)),
            full((CONV_WIDTH, D_XBC)),
            full((1, D_XBC)),
            full((1, LANES)),
            full((1, LANES)),
            full((1, D_SSM)),
            full((1, D_SSM)),
        ],
        out_specs=pl.BlockSpec((L, D_SSM), lambda c: (c, 0)),
        out_shape=jax.ShapeDtypeStruct((s, D_SSM), BF16),
        scratch_shapes=[pltpu.VMEM((L + 8, D_XBC), F32),
                        pltpu.VMEM((N_SSM_HEADS // 2, SSM_STATE, LANES), F32)],
        compiler_params=_params("arbitrary"),
        name="ssd_mixer",
    )(proj, proj, proj, dt_raw, conv_w, conv_b, dt_bias, a_log, d_skip_x, norm_w)


def _out_proj_kernel(a_ref, b_ref, wa_ref, wb_ref, x_ref, o_ref):
    acc = jnp.dot(a_ref[...], wa_ref[...], preferred_element_type=F32)
    acc = acc + jnp.dot(b_ref[...], wb_ref[...], preferred_element_type=F32)
    o_ref[...] = x_ref[...] + acc


def _out_proj(a, b, w, x, tm, tn):
    s = x.shape[0]
    ka, kb = a.shape[1], b.shape[1]
    return pl.pallas_call(
        _out_proj_kernel,
        grid=(s // tm, D_MODEL // tn),
        in_specs=[
            pl.BlockSpec((tm, ka), lambda i, j: (i, 0)),
            pl.BlockSpec((tm, kb), lambda i, j: (i, 0)),
            pl.BlockSpec((ka, tn), lambda i, j: (0, j)),
            pl.BlockSpec((kb, tn), lambda i, j: (1, j)),
            pl.BlockSpec((tm, tn), lambda i, j: (i, j)),
        ],
        out_specs=pl.BlockSpec((tm, tn), lambda i, j: (i, j)),
        out_shape=jax.ShapeDtypeStruct((s, D_MODEL), F32),
        compiler_params=_params("parallel", "arbitrary"),
        name="out_proj",
    )(a, b, w, w, x)


def _ffn_kernel(x_ref, nw_ref, wg_ref, wu_ref, wd_ref, fw_ref, o_ref, h_ref, *, final_norm):
    f = pl.program_id(1)

    @pl.when(f == 0)
    def _():
        xf = x_ref[...]
        ms = jnp.mean(xf * xf, axis=-1, keepdims=True)
        h_ref[...] = (xf * lax.rsqrt(ms + EPS) * nw_ref[...]).astype(BF16)
        o_ref[...] = xf

    h = h_ref[...]
    g = jnp.dot(h, wg_ref[...], preferred_element_type=F32)
    u = jnp.dot(h, wu_ref[...], preferred_element_type=F32)
    act = (_silu(g) * u).astype(BF16)
    o_ref[...] += jnp.dot(act, wd_ref[...], preferred_element_type=F32)

    if final_norm:
        @pl.when(f == pl.num_programs(1) - 1)
        def _():
            xf = o_ref[...]
            ms = jnp.mean(xf * xf, axis=-1, keepdims=True)
            o_ref[...] = xf * lax.rsqrt(ms + EPS) * fw_ref[...]


def _ffn(x, nw, w_gu, w_d, final_w, final_norm, tm, tf):
    s, d = x.shape
    nf = D_FF // tf
    kern = functools.partial(_ffn_kernel, final_norm=final_norm)
    return pl.pallas_call(
        kern,
        grid=(s // tm, nf),
        in_specs=[
            pl.BlockSpec((tm, d), lambda i, f: (i, 0)),
            pl.BlockSpec((1, d), lambda i, f: (0, 0)),
            pl.BlockSpec((d, tf), lambda i, f: (0, f)),
            pl.BlockSpec((d, tf), lambda i, f: (0, nf + f)),
            pl.BlockSpec((tf, d), lambda i, f: (f, 0)),
            pl.BlockSpec((1, d), lambda i, f: (0, 0)),
        ],
        out_specs=pl.BlockSpec((tm, d), lambda i, f: (i, 0)),
        out_shape=jax.ShapeDtypeStruct((s, d), F32),
        scratch_shapes=[pltpu.VMEM((tm, d), BF16)],
        compiler_params=_params("parallel", "arbitrary"),
        name="ffn",
    )(x, nw, w_gu, w_gu, w_d, final_w)


def _alibi_slopes(n_heads):
    start = 2.0 ** (-8.0 / n_heads)
    return start ** jnp.arange(1, n_heads + 1, dtype=F32)


def kernel(x, norm_mix_w, w_in, diff_lambda, subln_w, conv_w, conv_b, dt_bias, a_log,
           d_skip, ssm_norm_w, w_out, norm_ffn_w, w_gate_up, w_down, norm_final_w):
    b, s_len, _ = x.shape
    assert b == 1
    depth = w_in.shape[0]
    n_qkvzx = 3 * D_ATTN + D_SSM + D_XBC
    n_dt = w_in.shape[2] - n_qkvzx
    tm = min(512, s_len)
    tq = min(512, s_len)

    slopes = _alibi_slopes(N_ATTN_HEADS)
    q_scale = ATTN_HEAD_DIM ** -0.5 * LOG2E
    colscale = jnp.concatenate([jnp.full((1, D_ATTN), q_scale, F32),
                                jnp.ones((1, n_qkvzx - D_ATTN), F32)], axis=1)
    ones_dt = jnp.ones((1, LANES), F32)
    pad_l = lambda v: jnp.pad(v.astype(F32), (0, LANES - v.shape[0]))[None, :]

    xc = x[0]
    for l in range(depth):
        w_main = w_in[l, :, :n_qkvzx].astype(BF16)
        w_dt = jnp.pad(w_in[l, :, n_qkvzx:], ((0, 0), (0, LANES - n_dt))).astype(BF16)
        nw = norm_mix_w[l][None, :]
        proj = _rms_matmul(xc, nw, w_main, colscale, BF16, tm, 1024)
        dt_raw = _rms_matmul(xc, nw, w_dt, ones_dt, F32, tm, LANES)

        lam_init = 0.8 - 0.6 * math.exp(-0.3 * l)
        attn_out = _diff_attention(proj, slopes, diff_lambda[l], subln_w[l][:, None],
                                   lam_init, tq)
        ssm_out = _ssd_mixer(proj, dt_raw, conv_w[l], conv_b[l][None, :], pad_l(dt_bias[l]),
                             pad_l(a_log[l]), jnp.repeat(d_skip[l], SSM_HEAD_DIM)[None, :],
                             ssm_norm_w[l][None, :])
        xc = _out_proj(attn_out, ssm_out, w_out[l].astype(BF16), xc, tm, 1024)
        xc = _ffn(xc, norm_ffn_w[l][None, :], w_gate_up[l].astype(BF16),
                  w_down[l].astype(BF16), norm_final_w[None, :], l == depth - 1, tm, 512)
    return xc[None]
```

```python
import functools
import math

import jax
import jax.numpy as jnp
from jax import lax
from jax.experimental import pallas as pl
from jax.experimental.pallas import tpu as pltpu

D_MODEL = 2048
CHUNK = 64
D_ATTN = D_MODEL // 2
D_SSM = D_MODEL - D_ATTN
ATTN_HEAD_DIM = 64
ATTN_VDIM = 2 * ATTN_HEAD_DIM
N_ATTN_HEADS = D_ATTN // ATTN_VDIM
SSM_HEAD_DIM = 64
N_SSM_HEADS = D_SSM // SSM_HEAD_DIM
SSM_STATE = 128
SSM_GROUPS = 4
CONV_WIDTH = 4
SSD_CHUNK = 128
D_XBC = D_SSM + 2 * SSM_GROUPS * SSM_STATE
D_FF = ((8 * D_MODEL // 3 + 255) // 256) * 256
EPS = 1e-5

LANES = 128
LOG2E = math.log2(math.e)
NEG_BIG = -1e30
VMEM_LIMIT = 56 * 1024 * 1024
ATTN_HEADS_PER_STEP = 2

F32 = jnp.float32
BF16 = jnp.bfloat16


def _params(*sem):
    return pltpu.CompilerParams(dimension_semantics=sem, vmem_limit_bytes=VMEM_LIMIT)


def _rms_matmul_kernel(x_ref, nw_ref, w_ref, cs_ref, o_ref, h_ref):
    @pl.when(pl.program_id(1) == 0)
    def _():
        xf = x_ref[...]
        ms = jnp.mean(xf * xf, axis=-1, keepdims=True)
        h_ref[...] = (xf * lax.rsqrt(ms + EPS) * nw_ref[...]).astype(BF16)

    acc = jnp.dot(h_ref[...], w_ref[...], preferred_element_type=F32)
    o_ref[...] = (acc * cs_ref[...]).astype(o_ref.dtype)


def _rms_matmul(x, nw, w, colscale, out_dtype, tm, tn):
    s, d = x.shape
    n = w.shape[1]
    return pl.pallas_call(
        _rms_matmul_kernel,
        grid=(s // tm, n // tn),
        in_specs=[
            pl.BlockSpec((tm, d), lambda i, j: (i, 0)),
            pl.BlockSpec((1, d), lambda i, j: (0, 0)),
            pl.BlockSpec((d, tn), lambda i, j: (0, j)),
            pl.BlockSpec((1, tn), lambda i, j: (0, j)),
        ],
        out_specs=pl.BlockSpec((tm, tn), lambda i, j: (i, j)),
        out_shape=jax.ShapeDtypeStruct((s, n), out_dtype),
        scratch_shapes=[pltpu.VMEM((tm, d), BF16)],
        compiler_params=_params("parallel", "arbitrary"),
        name="rms_matmul",
    )(x, nw, w, colscale)


def _rms_matmul_t_kernel(x_ref, nw_ref, wt_ref, o_ref):
    xf = x_ref[...]
    ms = jnp.mean(xf * xf, axis=-1, keepdims=True)
    h = (xf * lax.rsqrt(ms + EPS) * nw_ref[...]).astype(BF16)
    o_ref[...] = lax.dot_general(wt_ref[...], h, (((1,), (1,)), ((), ())),
                                 preferred_element_type=F32).astype(o_ref.dtype)


def _rms_matmul_t(x, nw, wt, out_dtype, tm):
    s, d = x.shape
    n = wt.shape[0]
    return pl.pallas_call(
        _rms_matmul_t_kernel,
        grid=(s // tm,),
        in_specs=[
            pl.BlockSpec((tm, d), lambda i: (i, 0)),
            pl.BlockSpec((1, d), lambda i: (0, 0)),
            pl.BlockSpec((n, d), lambda i: (0, 0)),
        ],
        out_specs=pl.BlockSpec((n, tm), lambda i: (0, i)),
        out_shape=jax.ShapeDtypeStruct((n, s), out_dtype),
        compiler_params=_params("parallel"),
        name="rms_matmul_t",
    )(x, nw, wt)


ONES_ROWS = 16
ACC_ROWS = ATTN_VDIM + ONES_ROWS


def _split3(a):
    hi = a.astype(BF16)
    r1 = a - hi.astype(F32)
    mid = r1.astype(BF16)
    lo = (r1 - mid.astype(F32)).astype(BF16)
    return hi, mid, lo


def _attn_kernel(slopes_ref, dl_ref, q_ref, k_ref, vt_ref, sw_ref, o_ref,
                 acc_ref, s_ref, p_ref, *, tq, hps, lam_init):
    g = pl.program_id(0)
    i = pl.program_id(1)
    heads = range(hps)
    hm = [(hh, j) for hh in heads for j in range(2)]
    cs = [slopes_ref[g * hps + hh] * LOG2E for hh in heads]

    shape = (tq, LANES)
    lane = lax.broadcasted_iota(jnp.int32, shape, 1)
    zero = jnp.zeros(shape, BF16)
    krow = lax.broadcasted_iota(jnp.int32, shape, 0)
    k_split = jnp.where((lane & 1) == 0, (krow >> 4) << 4, krow & 15)
    k_bias = jnp.where(lane < 6, k_split, 0).astype(F32).astype(BF16)
    q_maps = {}
    for hh in heads:
        q = q_ref[:, hh * LANES:(hh + 1) * LANES]
        c_terms = _split3(jnp.full(shape, cs[hh], F32))
        q_bias = jnp.zeros(shape, F32)
        for t in range(3):
            q_bias = jnp.where((lane >> 1) == t, c_terms[t].astype(F32), q_bias)
        q_bias = q_bias.astype(BF16)
        q_maps[hh, 0] = jnp.concatenate([jnp.where(lane < ATTN_HEAD_DIM, q, zero), q_bias], axis=1)
        q_maps[hh, 1] = jnp.concatenate([jnp.where(lane >= ATTN_HEAD_DIM, q, zero), q_bias], axis=1)

    acc_ref[...] = jnp.zeros_like(acc_ref)
    p_ref[1] = jnp.zeros(p_ref.shape[1:], BF16)
    ones_rows = jnp.ones((ONES_ROWS, tq), BF16)

    def scores(blk, slot):
        k0 = pl.multiple_of(blk * tq, tq)
        for hh in heads:
            kblk = jnp.concatenate([k_ref[pl.ds(k0, tq), hh * LANES:(hh + 1) * LANES], k_bias],
                                   axis=1)
            for j in range(2):
                s_ref[slot, hh, j] = lax.dot_general(kblk, q_maps[hh, j], (((1,), (1,)), ((), ())),
                                                     preferred_element_type=F32)

    def softmax(slot, bias, offs, ms):
        new_ms, alphas = {}, {}
        for hh, j in hm:
            s = s_ref[slot, hh, j]
            if bias is not None:
                s = s + bias[hh]
            m_new = jnp.maximum(ms[hh, j], jnp.max(s, axis=0, keepdims=True) + offs[hh])
            alphas[hh, j] = jnp.exp2(ms[hh, j] - m_new)
            p_ref[slot, hh, j] = jnp.exp2(s - (m_new - offs[hh])).astype(BF16)
            new_ms[hh, j] = m_new
        return new_ms, alphas

    def values(blk, slot, alphas):
        k0 = pl.multiple_of(blk * tq, tq)
        for hh in heads:
            lhs = jnp.concatenate([vt_ref[hh * ATTN_VDIM:(hh + 1) * ATTN_VDIM, pl.ds(k0, tq)],
                                   ones_rows], axis=0)
            for j in range(2):
                pv = jnp.dot(lhs, p_ref[slot, hh, j], preferred_element_type=F32)
                acc_ref[hh, j] = alphas[hh, j] * acc_ref[hh, j] + pv

    init_m = {k: jnp.full((1, tq), NEG_BIG, F32) for k in hm}
    init_a = {k: jnp.ones((1, tq), F32) for k in hm}
    scores(0, 0)

    def past_block(t, slot, carry):
        ms, alphas = carry
        values(jnp.maximum(t - 1, 0), 1 - slot, alphas)
        dist = ((i - t) * tq).astype(F32)
        ms, alphas = softmax(slot, None, [-cs[hh] * dist for hh in heads], ms)
        scores(t + 1, 1 - slot)
        return ms, alphas

    def past_step(t, carry):
        return lax.cond((t & 1) == 0, functools.partial(past_block, t, 0),
                        functools.partial(past_block, t, 1), carry)

    carry = lax.fori_loop(0, i, past_step, (init_m, init_a))

    dl = dl_ref[...]
    lam = (jnp.exp(jnp.sum(dl[0:1] * dl[1:2], axis=-1, keepdims=True))
           - jnp.exp(jnp.sum(dl[2:3] * dl[3:4], axis=-1, keepdims=True)) + lam_init)

    def finish(slot):
        ms, alphas = carry
        values(jnp.maximum(i - 1, 0), 1 - slot, alphas)
        kl = lax.broadcasted_iota(jnp.int32, (tq, tq), 0)
        ql = lax.broadcasted_iota(jnp.int32, (tq, tq), 1)
        allowed = (kl // CHUNK) <= (ql // CHUNK)
        ahead = jnp.maximum(kl - ql, 0).astype(F32)
        bias = [jnp.where(allowed, (-2.0 * cs[hh]) * ahead, NEG_BIG) for hh in heads]
        ms, alphas = softmax(slot, bias, [0.0] * hps, ms)
        values(i, slot, alphas)
        for hh in heads:
            a1, a2 = acc_ref[hh, 0], acc_ref[hh, 1]
            o = (a1[:ATTN_VDIM] * (1.0 / a1[ATTN_VDIM:ATTN_VDIM + 1])
                 - lam * (a2[:ATTN_VDIM] * (1.0 / a2[ATTN_VDIM:ATTN_VDIM + 1])))
            ms_o = jnp.mean(o * o, axis=0, keepdims=True)
            y = (o * lax.rsqrt(ms_o + EPS) * sw_ref[...]) * (1.0 - lam_init)
            o_ref[:, hh * LANES:(hh + 1) * LANES] = y.T.astype(o_ref.dtype)

    for parity in range(2):
        pl.when((i & 1) == parity)(functools.partial(finish, parity))


def _diff_attention(proj, vt, slopes, dl, subw_col, lam_init, tq, hps):
    s = proj.shape[0]
    w = hps * LANES
    kern = functools.partial(_attn_kernel, tq=tq, hps=hps, lam_init=lam_init)
    return pl.pallas_call(
        kern,
        grid=(N_ATTN_HEADS // hps, s // tq),
        in_specs=[
            pl.BlockSpec(memory_space=pltpu.SMEM),
            pl.BlockSpec((4, ATTN_HEAD_DIM), lambda g, i: (0, 0)),
            pl.BlockSpec((tq, w), lambda g, i: (i, g)),
            pl.BlockSpec((s, w), lambda g, i: (0, D_ATTN // w + g)),
            pl.BlockSpec((hps * ATTN_VDIM, s), lambda g, i: (g, 0)),
            pl.BlockSpec((ATTN_VDIM, 1), lambda g, i: (0, 0)),
        ],
        out_specs=pl.BlockSpec((tq, w), lambda g, i: (i, g)),
        out_shape=jax.ShapeDtypeStruct((s, D_ATTN), BF16),
        scratch_shapes=[pltpu.VMEM((hps, 2, ACC_ROWS, tq), F32),
                        pltpu.VMEM((2, hps, 2, tq, tq), F32),
                        pltpu.VMEM((2, hps, 2, tq, tq), BF16)],
        compiler_params=_params("parallel", "arbitrary"),
        name="diff_attention",
    )(slopes, dl, proj, proj, vt, subw_col)


def _dot3(a, b_bf16):
    out = None
    for t in _split3(a):
        d = jnp.dot(t, b_bf16, preferred_element_type=F32)
        out = d if out is None else out + d
    return out


def _silu(x):
    return x * (1.0 / (1.0 + jnp.exp(-x)))


def _ssd_kernel(z_ref, xr_ref, bcr_ref, dtr_ref, cw_ref, cb_ref, dtb_ref, alog_ref,
                dskip_ref, nw_ref, o_ref, ext_ref, state_ref):
    L = SSD_CHUNK
    c = pl.program_id(0)

    @pl.when(c == 0)
    def _():
        ext_ref[0:8, :] = jnp.zeros((8, D_XBC), F32)
        state_ref[...] = jnp.zeros_like(state_ref)

    ext_ref[8:8 + L, 0:D_SSM] = xr_ref[...].astype(F32)
    ext_ref[8:8 + L, D_SSM:D_XBC] = bcr_ref[...].astype(F32)
    conv = cb_ref[...]
    for t in range(CONV_WIDTH):
        conv = conv + cw_ref[t:t + 1, :] * ext_ref[5 + t:5 + t + L, :]
    ext_ref[0:8, :] = ext_ref[L:L + 8, :]
    xbc = _silu(conv)
    xs = xbc[:, :D_SSM]

    dt = dtr_ref[...] + dtb_ref[...]
    dt = jnp.maximum(dt, 0.0) + jnp.log1p(jnp.exp(-jnp.abs(dt)))
    a = -jnp.exp(alog_ref[...]) * dt
    row = lax.broadcasted_iota(jnp.int32, (L, L), 0)
    col = lax.broadcasted_iota(jnp.int32, (L, L), 1)
    causal = col <= row
    tril = jnp.where(causal, 1.0, 0.0).astype(BF16)
    a_cs = None
    for t in _split3(a):
        d = jnp.dot(tril, t, preferred_element_type=F32)
        a_cs = d if a_cs is None else a_cs + d
    a_cs_t = a_cs.T
    a_end = a_cs[L - 1:L, :]

    hrow = lax.broadcasted_iota(jnp.int32, (LANES, D_SSM), 0)
    ccol = lax.broadcasted_iota(jnp.int32, (LANES, D_SSM), 1)
    expand = jnp.where(ccol // SSM_HEAD_DIM == hrow, 1.0, 0.0).astype(BF16)
    dt_x = _dot3(dt, expand)
    grow_x = _dot3(jnp.exp(a_cs), expand)
    tail_x = _dot3(jnp.exp(a_end - a_cs), expand)
    end_x = grow_x[L - 1:L, :]

    xd = xs * dt_x
    xd_b = xd.astype(BF16)
    xdt_b = (xd * tail_x).astype(BF16)
    lane = lax.broadcasted_iota(jnp.int32, (L, LANES), 1)
    lo_half = lane < SSM_HEAD_DIM
    zeros_b = jnp.zeros((L, LANES), BF16)

    y_parts = []
    for g in range(SSM_GROUPS):
        b_g = xbc[:, D_SSM + g * SSM_STATE:D_SSM + (g + 1) * SSM_STATE].astype(BF16)
        c_g = xbc[:, D_SSM + (SSM_GROUPS + g) * SSM_STATE:
                  D_SSM + (SSM_GROUPS + g + 1) * SSM_STATE].astype(BF16)
        cb = lax.dot_general(c_g, b_g, (((1,), (1,)), ((), ())),
                             preferred_element_type=F32)
        for pr in range(2):
            pair = 2 * g + pr
            lanes = slice(pair * LANES, (pair + 1) * LANES)
            xd_p = xd_b[:, lanes]
            y_p = None
            for hh in range(2):
                head = 2 * pair + hh
                seg = a_cs[:, head:head + 1] - a_cs_t[head:head + 1, :]
                decay = jnp.exp(jnp.where(causal, seg, NEG_BIG))
                w = (cb * decay).astype(BF16)
                xh = jnp.where(lo_half if hh == 0 else ~lo_half, xd_p, zeros_b)
                d = jnp.dot(w, xh, preferred_element_type=F32)
                y_p = d if y_p is None else y_p + d
            st = state_ref[pair]
            y_off = jnp.dot(c_g, st.astype(BF16), preferred_element_type=F32)
            y_parts.append(y_p + y_off * grow_x[:, lanes])
            new = lax.dot_general(b_g, xdt_b[:, lanes], (((0,), (0,)), ((), ())),
                                  preferred_element_type=F32)
            state_ref[pair] = st * end_x[:, lanes] + new

    y = jnp.concatenate(y_parts, axis=-1) + xs * dskip_ref[...]
    gated = y * _silu(z_ref[...].astype(F32))
    ms = jnp.mean(gated * gated, axis=-1, keepdims=True)
    o_ref[...] = (gated * lax.rsqrt(ms + EPS) * nw_ref[...]).astype(o_ref.dtype)


def _ssd_mixer(proj, dt_raw, conv_w, conv_b, dt_bias, a_log, d_skip_x, norm_w):
    s = proj.shape[0]
    L = SSD_CHUNK
    full = lambda shape: pl.BlockSpec(shape, lambda c: (0, 0))
    return pl.pallas_call(
        _ssd_kernel,
        grid=(s // L,),
        in_specs=[
            pl.BlockSpec((L, D_SSM), lambda c: (c, 2)),
            pl.BlockSpec((L, D_SSM), lambda c: (c, 3)),
            pl.BlockSpec((L, D_SSM), lambda c: (c, 4)),
            pl.BlockSpec((L, LANES), lambda c: (c, 0)),
            full((CONV_WIDTH, D_XBC)),
            full((1, D_XBC)),
            full((1, LANES)),
            full((1, LANES)),
            full((1, D_SSM)),
            full((1, D_SSM)),
        ],
        out_specs=pl.BlockSpec((L, D_SSM), lambda c: (c, 0)),
        out_shape=jax.ShapeDtypeStruct((s, D_SSM), BF16),
        scratch_shapes=[pltpu.VMEM((L + 8, D_XBC), F32),
                        pltpu.VMEM((N_SSM_HEADS // 2, SSM_STATE, LANES), F32)],
        compiler_params=_params("arbitrary"),
        name="ssd_mixer",
    )(proj, proj, proj, dt_raw, conv_w, conv_b, dt_bias, a_log, d_skip_x, norm_w)


def _out_proj_kernel(a_ref, b_ref, wa_ref, wb_ref, x_ref, o_ref):
    acc = jnp.dot(a_ref[...], wa_ref[...], preferred_element_type=F32)
    acc = acc + jnp.dot(b_ref[...], wb_ref[...], preferred_element_type=F32)
    o_ref[...] = x_ref[...] + acc


def _out_proj(a, b, w, x, tm, tn):
    s = x.shape[0]
    ka, kb = a.shape[1], b.shape[1]
    return pl.pallas_call(
        _out_proj_kernel,
        grid=(s // tm, D_MODEL // tn),
        in_specs=[
            pl.BlockSpec((tm, ka), lambda i, j: (i, 0)),
            pl.BlockSpec((tm, kb), lambda i, j: (i, 0)),
            pl.BlockSpec((ka, tn), lambda i, j: (0, j)),
            pl.BlockSpec((kb, tn), lambda i, j: (1, j)),
            pl.BlockSpec((tm, tn), lambda i, j: (i, j)),
        ],
        out_specs=pl.BlockSpec((tm, tn), lambda i, j: (i, j)),
        out_shape=jax.ShapeDtypeStruct((s, D_MODEL), F32),
        compiler_params=_params("parallel", "arbitrary"),
        name="out_proj",
    )(a, b, w, w, x)


def _ffn_kernel(x_ref, nw_ref, wg_ref, wu_ref, wd_ref, fw_ref, o_ref, h_ref, *, final_norm):
    f = pl.program_id(1)

    @pl.when(f == 0)
    def _():
        xf = x_ref[...]
        ms = jnp.mean(xf * xf, axis=-1, keepdims=True)
        h_ref[...] = (xf * lax.rsqrt(ms + EPS) * nw_ref[...]).astype(BF16)
        o_ref[...] = xf

    h = h_ref[...]
    g = jnp.dot(h, wg_ref[...], preferred_element_type=F32)
    u = jnp.dot(h, wu_ref[...], preferred_element_type=F32)
    act = (_silu(g) * u).astype(BF16)
    o_ref[...] += jnp.dot(act, wd_ref[...], preferred_element_type=F32)

    if final_norm:
        @pl.when(f == pl.num_programs(1) - 1)
        def _():
            xf = o_ref[...]
            ms = jnp.mean(xf * xf, axis=-1, keepdims=True)
            o_ref[...] = xf * lax.rsqrt(ms + EPS) * fw_ref[...]


def _ffn(x, nw, w_gu, w_d, final_w, final_norm, tm, tf):
    s, d = x.shape
    nf = D_FF // tf
    kern = functools.partial(_ffn_kernel, final_norm=final_norm)
    return pl.pallas_call(
        kern,
        grid=(s // tm, nf),
        in_specs=[
            pl.BlockSpec((tm, d), lambda i, f: (i, 0)),
            pl.BlockSpec((1, d), lambda i, f: (0, 0)),
            pl.BlockSpec((d, tf), lambda i, f: (0, f)),
            pl.BlockSpec((d, tf), lambda i, f: (0, nf + f)),
            pl.BlockSpec((tf, d), lambda i, f: (f, 0)),
            pl.BlockSpec((1, d), lambda i, f: (0, 0)),
        ],
        out_specs=pl.BlockSpec((tm, d), lambda i, f: (i, 0)),
        out_shape=jax.ShapeDtypeStruct((s, d), F32),
        scratch_shapes=[pltpu.VMEM((tm, d), BF16)],
        compiler_params=_params("parallel", "arbitrary"),
        name="ffn",
    )(x, nw, w_gu, w_gu, w_d, final_w)


def _alibi_slopes(n_heads):
    start = 2.0 ** (-8.0 / n_heads)
    return start ** jnp.arange(1, n_heads + 1, dtype=F32)


def kernel(x, norm_mix_w, w_in, diff_lambda, subln_w, conv_w, conv_b, dt_bias, a_log,
           d_skip, ssm_norm_w, w_out, norm_ffn_w, w_gate_up, w_down, norm_final_w):
    b, s_len, _ = x.shape
    assert b == 1
    depth = w_in.shape[0]
    o_v, o_z = 2 * D_ATTN, 3 * D_ATTN
    o_dt = o_z + D_SSM + D_XBC
    n_dt = w_in.shape[2] - o_dt
    tm = min(512, s_len)
    tq = min(512, s_len)

    slopes = _alibi_slopes(N_ATTN_HEADS)
    q_scale = ATTN_HEAD_DIM ** -0.5 * LOG2E
    colscale = jnp.concatenate([jnp.full((1, D_ATTN), q_scale, F32),
                                jnp.ones((1, o_dt - 2 * D_ATTN), F32)], axis=1)
    ones_dt = jnp.ones((1, LANES), F32)
    pad_l = lambda v: jnp.pad(v.astype(F32), (0, LANES - v.shape[0]))[None, :]

    xc = x[0]
    for l in range(depth):
        w_main = jnp.concatenate([w_in[l, :, :o_v], w_in[l, :, o_z:o_dt]], axis=1).astype(BF16)
        w_vt = w_in[l, :, o_v:o_z].T.astype(BF16)
        w_dt = jnp.pad(w_in[l, :, o_dt:], ((0, 0), (0, LANES - n_dt))).astype(BF16)
        nw = norm_mix_w[l][None, :]
        proj = _rms_matmul(xc, nw, w_main, colscale, BF16, tm, 1024)
        vt = _rms_matmul_t(xc, nw, w_vt, BF16, tm)
        dt_raw = _rms_matmul(xc, nw, w_dt, ones_dt, F32, tm, LANES)

        lam_init = 0.8 - 0.6 * math.exp(-0.3 * l)
        attn_out = _diff_attention(proj, vt, slopes, diff_lambda[l], subln_w[l][:, None],
                                   lam_init, tq, ATTN_HEADS_PER_STEP)
        ssm_out = _ssd_mixer(proj, dt_raw, conv_w[l], conv_b[l][None, :], pad_l(dt_bias[l]),
                             pad_l(a_log[l]), jnp.repeat(d_skip[l], SSM_HEAD_DIM)[None, :],
                             ssm_norm_w[l][None, :])
        xc = _out_proj(attn_out, ssm_out, w_out[l].astype(BF16), xc, tm, 1024)
        xc = _ffn(xc, norm_ffn_w[l][None, :], w_gate_up[l].astype(BF16),
                  w_down[l].astype(BF16), norm_final_w[None, :], l == depth - 1, tm, 512)
    return xc[None]
```

```python
import functools
import math

import jax
import jax.numpy as jnp
from jax import lax
from jax.experimental import pallas as pl
from jax.experimental.pallas import tpu as pltpu

D_MODEL = 2048
CHUNK = 64
D_ATTN = D_MODEL // 2
D_SSM = D_MODEL - D_ATTN
ATTN_HEAD_DIM = 64
ATTN_VDIM = 2 * ATTN_HEAD_DIM
N_ATTN_HEADS = D_ATTN // ATTN_VDIM
SSM_HEAD_DIM = 64
N_SSM_HEADS = D_SSM // SSM_HEAD_DIM
SSM_STATE = 128
SSM_GROUPS = 4
CONV_WIDTH = 4
SSD_CHUNK = 128
D_XBC = D_SSM + 2 * SSM_GROUPS * SSM_STATE
D_FF = ((8 * D_MODEL // 3 + 255) // 256) * 256
EPS = 1e-5

LANES = 128
LOG2E = math.log2(math.e)
NEG_BIG = -1e30
VMEM_LIMIT = 56 * 1024 * 1024
ATTN_HEADS_PER_STEP = 2

F32 = jnp.float32
BF16 = jnp.bfloat16


def _params(*sem):
    return pltpu.CompilerParams(dimension_semantics=sem, vmem_limit_bytes=VMEM_LIMIT)


def _rms_matmul_kernel(x_ref, nw_ref, w_ref, cs_ref, o_ref, h_ref):
    @pl.when(pl.program_id(1) == 0)
    def _():
        xf = x_ref[...]
        ms = jnp.mean(xf * xf, axis=-1, keepdims=True)
        h_ref[...] = (xf * lax.rsqrt(ms + EPS) * nw_ref[...]).astype(BF16)

    acc = jnp.dot(h_ref[...], w_ref[...], preferred_element_type=F32)
    o_ref[...] = (acc * cs_ref[...]).astype(o_ref.dtype)


def _rms_matmul(x, nw, w, colscale, out_dtype, tm, tn):
    s, d = x.shape
    n = w.shape[1]
    return pl.pallas_call(
        _rms_matmul_kernel,
        grid=(s // tm, n // tn),
        in_specs=[
            pl.BlockSpec((tm, d), lambda i, j: (i, 0)),
            pl.BlockSpec((1, d), lambda i, j: (0, 0)),
            pl.BlockSpec((d, tn), lambda i, j: (0, j)),
            pl.BlockSpec((1, tn), lambda i, j: (0, j)),
        ],
        out_specs=pl.BlockSpec((tm, tn), lambda i, j: (i, j)),
        out_shape=jax.ShapeDtypeStruct((s, n), out_dtype),
        scratch_shapes=[pltpu.VMEM((tm, d), BF16)],
        compiler_params=_params("parallel", "arbitrary"),
        name="rms_matmul",
    )(x, nw, w, colscale)


def _rms_matmul_t_kernel(x_ref, nw_ref, wt_ref, o_ref):
    xf = x_ref[...]
    ms = jnp.mean(xf * xf, axis=-1, keepdims=True)
    h = (xf * lax.rsqrt(ms + EPS) * nw_ref[...]).astype(BF16)
    o_ref[...] = lax.dot_general(wt_ref[...], h, (((1,), (1,)), ((), ())),
                                 preferred_element_type=F32).astype(o_ref.dtype)


def _rms_matmul_t(x, nw, wt, out_dtype, tm):
    s, d = x.shape
    n = wt.shape[0]
    return pl.pallas_call(
        _rms_matmul_t_kernel,
        grid=(s // tm,),
        in_specs=[
            pl.BlockSpec((tm, d), lambda i: (i, 0)),
            pl.BlockSpec((1, d), lambda i: (0, 0)),
            pl.BlockSpec((n, d), lambda i: (0, 0)),
        ],
        out_specs=pl.BlockSpec((n, tm), lambda i: (0, i)),
        out_shape=jax.ShapeDtypeStruct((n, s), out_dtype),
        compiler_params=_params("parallel"),
        name="rms_matmul_t",
    )(x, nw, wt)


ONES_ROWS = 16
ACC_ROWS = ATTN_VDIM + ONES_ROWS


def _split3(a):
    hi = a.astype(BF16)
    r1 = a - hi.astype(F32)
    mid = r1.astype(BF16)
    lo = (r1 - mid.astype(F32)).astype(BF16)
    return hi, mid, lo


def _attn_kernel(slopes_ref, dl_ref, q_ref, k_ref, vt_ref, sw_ref, o_ref,
                 acc_ref, s_ref, p_ref, *, tq, hps, lam_init):
    g = pl.program_id(0)
    i = pl.program_id(1)
    heads = range(hps)
    hm = [(hh, j) for hh in heads for j in range(2)]
    cs = [slopes_ref[g * hps + hh] * LOG2E for hh in heads]

    shape = (tq, LANES)
    lane = lax.broadcasted_iota(jnp.int32, shape, 1)
    zero = jnp.zeros(shape, BF16)
    krow = lax.broadcasted_iota(jnp.int32, shape, 0)
    k_split = jnp.where((lane & 1) == 0, (krow >> 4) << 4, krow & 15)
    k_bias = jnp.where(lane < 6, k_split, 0).astype(F32).astype(BF16)
    q_maps = {}
    for hh in heads:
        q = q_ref[:, hh * LANES:(hh + 1) * LANES]
        c_terms = _split3(jnp.full(shape, cs[hh], F32))
        q_bias = jnp.zeros(shape, F32)
        for t in range(3):
            q_bias = jnp.where((lane >> 1) == t, c_terms[t].astype(F32), q_bias)
        q_bias = q_bias.astype(BF16)
        q_maps[hh, 0] = jnp.concatenate([jnp.where(lane < ATTN_HEAD_DIM, q, zero), q_bias], axis=1)
        q_maps[hh, 1] = jnp.concatenate([jnp.where(lane >= ATTN_HEAD_DIM, q, zero), q_bias], axis=1)

    acc_ref[...] = jnp.zeros_like(acc_ref)
    p_ref[1] = jnp.zeros(p_ref.shape[1:], BF16)
    ones_rows = jnp.ones((ONES_ROWS, tq), BF16)

    def scores(blk, slot):
        k0 = pl.multiple_of(blk * tq, tq)
        col_max = {}
        for hh in heads:
            kblk = jnp.concatenate([k_ref[pl.ds(k0, tq), hh * LANES:(hh + 1) * LANES], k_bias],
                                   axis=1)
            for j in range(2):
                s = lax.dot_general(kblk, q_maps[hh, j], (((1,), (1,)), ((), ())),
                                    preferred_element_type=F32)
                s_ref[slot, hh, j] = s
                col_max[hh, j] = jnp.max(s, axis=0, keepdims=True)
        return col_max

    def softmax(slot, bias, offs, ms, col_max):
        new_ms, alphas = {}, {}
        for hh, j in hm:
            s = s_ref[slot, hh, j]
            if bias is None:
                s_max = col_max[hh, j]
            else:
                s = s + bias[hh]
                s_max = jnp.max(s, axis=0, keepdims=True)
            m_new = jnp.maximum(ms[hh, j], s_max + offs[hh])
            alphas[hh, j] = jnp.exp2(ms[hh, j] - m_new)
            p_ref[slot, hh, j] = jnp.exp2(s - (m_new - offs[hh])).astype(BF16)
            new_ms[hh, j] = m_new
        return new_ms, alphas

    def values(blk, slot, alphas):
        k0 = pl.multiple_of(blk * tq, tq)
        for hh in heads:
            lhs = jnp.concatenate([vt_ref[hh * ATTN_VDIM:(hh + 1) * ATTN_VDIM, pl.ds(k0, tq)],
                                   ones_rows], axis=0)
            for j in range(2):
                pv = jnp.dot(lhs, p_ref[slot, hh, j], preferred_element_type=F32)
                acc_ref[hh, j] = alphas[hh, j] * acc_ref[hh, j] + pv

    init_m = {k: jnp.full((1, tq), NEG_BIG, F32) for k in hm}
    init_a = {k: jnp.ones((1, tq), F32) for k in hm}
    col_max0 = scores(0, 0)

    def past_block(t, slot, carry):
        ms, alphas, col_max = carry
        values(jnp.maximum(t - 1, 0), 1 - slot, alphas)
        dist = ((i - t) * tq).astype(F32)
        ms, alphas = softmax(slot, None, [-cs[hh] * dist for hh in heads], ms, col_max)
        col_max = scores(t + 1, 1 - slot)
        return ms, alphas, col_max

    def past_step(t, carry):
        return lax.cond((t & 1) == 0, functools.partial(past_block, t, 0),
                        functools.partial(past_block, t, 1), carry)

    carry = lax.fori_loop(0, i, past_step, (init_m, init_a, col_max0))

    dl = dl_ref[...]
    lam = (jnp.exp(jnp.sum(dl[0:1] * dl[1:2], axis=-1, keepdims=True))
           - jnp.exp(jnp.sum(dl[2:3] * dl[3:4], axis=-1, keepdims=True)) + lam_init)

    def finish(slot):
        ms, alphas, _ = carry
        values(jnp.maximum(i - 1, 0), 1 - slot, alphas)
        kl = lax.broadcasted_iota(jnp.int32, (tq, tq), 0)
        ql = lax.broadcasted_iota(jnp.int32, (tq, tq), 1)
        allowed = (kl // CHUNK) <= (ql // CHUNK)
        ahead = jnp.maximum(kl - ql, 0).astype(F32)
        bias = [jnp.where(allowed, (-2.0 * cs[hh]) * ahead, NEG_BIG) for hh in heads]
        ms, alphas = softmax(slot, bias, [0.0] * hps, ms, None)
        values(i, slot, alphas)
        for hh in heads:
            a1, a2 = acc_ref[hh, 0], acc_ref[hh, 1]
            o = (a1[:ATTN_VDIM] * (1.0 / a1[ATTN_VDIM:ATTN_VDIM + 1])
                 - lam * (a2[:ATTN_VDIM] * (1.0 / a2[ATTN_VDIM:ATTN_VDIM + 1])))
            ms_o = jnp.mean(o * o, axis=0, keepdims=True)
            y = (o * lax.rsqrt(ms_o + EPS) * sw_ref[...]) * (1.0 - lam_init)
            o_ref[:, hh * LANES:(hh + 1) * LANES] = y.T.astype(o_ref.dtype)

    for parity in range(2):
        pl.when((i & 1) == parity)(functools.partial(finish, parity))


def _diff_attention(proj, vt, slopes, dl, subw_col, lam_init, tq, hps):
    s = proj.shape[0]
    w = hps * LANES
    kern = functools.partial(_attn_kernel, tq=tq, hps=hps, lam_init=lam_init)
    return pl.pallas_call(
        kern,
        grid=(N_ATTN_HEADS // hps, s // tq),
        in_specs=[
            pl.BlockSpec(memory_space=pltpu.SMEM),
            pl.BlockSpec((4, ATTN_HEAD_DIM), lambda g, i: (0, 0)),
            pl.BlockSpec((tq, w), lambda g, i: (i, g)),
            pl.BlockSpec((s, w), lambda g, i: (0, D_ATTN // w + g)),
            pl.BlockSpec((hps * ATTN_VDIM, s), lambda g, i: (g, 0)),
            pl.BlockSpec((ATTN_VDIM, 1), lambda g, i: (0, 0)),
        ],
        out_specs=pl.BlockSpec((tq, w), lambda g, i: (i, g)),
        out_shape=jax.ShapeDtypeStruct((s, D_ATTN), BF16),
        scratch_shapes=[pltpu.VMEM((hps, 2, ACC_ROWS, tq), F32),
                        pltpu.VMEM((2, hps, 2, tq, tq), F32),
                        pltpu.VMEM((2, hps, 2, tq, tq), BF16)],
        compiler_params=_params("parallel", "arbitrary"),
        name="diff_attention",
    )(slopes, dl, proj, proj, vt, subw_col)


def _dot3(a, b_bf16):
    out = None
    for t in _split3(a):
        d = jnp.dot(t, b_bf16, preferred_element_type=F32)
        out = d if out is None else out + d
    return out


def _silu(x):
    return x * (1.0 / (1.0 + jnp.exp(-x)))


def _ssd_kernel(z_ref, xr_ref, bcr_ref, dtr_ref, cw_ref, cb_ref, dtb_ref, alog_ref,
                dskip_ref, nw_ref, o_ref, ext_ref, state_ref):
    L = SSD_CHUNK
    c = pl.program_id(0)

    @pl.when(c == 0)
    def _():
        ext_ref[0:8, :] = jnp.zeros((8, D_XBC), F32)
        state_ref[...] = jnp.zeros_like(state_ref)

    ext_ref[8:8 + L, 0:D_SSM] = xr_ref[...].astype(F32)
    ext_ref[8:8 + L, D_SSM:D_XBC] = bcr_ref[...].astype(F32)
    conv = cb_ref[...]
    for t in range(CONV_WIDTH):
        conv = conv + cw_ref[t:t + 1, :] * ext_ref[5 + t:5 + t + L, :]
    ext_ref[0:8, :] = ext_ref[L:L + 8, :]
    xbc = _silu(conv)
    xs = xbc[:, :D_SSM]

    dt = dtr_ref[...] + dtb_ref[...]
    dt = jnp.maximum(dt, 0.0) + jnp.log1p(jnp.exp(-jnp.abs(dt)))
    a = -jnp.exp(alog_ref[...]) * dt
    row = lax.broadcasted_iota(jnp.int32, (L, L), 0)
    col = lax.broadcasted_iota(jnp.int32, (L, L), 1)
    causal = col <= row
    tril = jnp.where(causal, 1.0, 0.0).astype(BF16)
    a_cs = None
    for t in _split3(a):
        d = jnp.dot(tril, t, preferred_element_type=F32)
        a_cs = d if a_cs is None else a_cs + d
    a_cs_t = a_cs.T
    a_end = a_cs[L - 1:L, :]

    hrow = lax.broadcasted_iota(jnp.int32, (LANES, D_SSM), 0)
    ccol = lax.broadcasted_iota(jnp.int32, (LANES, D_SSM), 1)
    expand = jnp.where(ccol // SSM_HEAD_DIM == hrow, 1.0, 0.0).astype(BF16)
    dt_x = _dot3(dt, expand)
    grow_x = _dot3(jnp.exp(a_cs), expand)
    tail_x = _dot3(jnp.exp(a_end - a_cs), expand)
    end_x = grow_x[L - 1:L, :]

    xd = xs * dt_x
    xd_b = xd.astype(BF16)
    xdt_b = (xd * tail_x).astype(BF16)
    lane = lax.broadcasted_iota(jnp.int32, (L, LANES), 1)
    lo_half = lane < SSM_HEAD_DIM
    zeros_b = jnp.zeros((L, LANES), BF16)

    y_parts = []
    for g in range(SSM_GROUPS):
        b_g = xbc[:, D_SSM + g * SSM_STATE:D_SSM + (g + 1) * SSM_STATE].astype(BF16)
        c_g = xbc[:, D_SSM + (SSM_GROUPS + g) * SSM_STATE:
                  D_SSM + (SSM_GROUPS + g + 1) * SSM_STATE].astype(BF16)
        cb = lax.dot_general(c_g, b_g, (((1,), (1,)), ((), ())),
                             preferred_element_type=F32)
        for pr in range(2):
            pair = 2 * g + pr
            lanes = slice(pair * LANES, (pair + 1) * LANES)
            xd_p = xd_b[:, lanes]
            y_p = None
            for hh in range(2):
                head = 2 * pair + hh
                seg = a_cs[:, head:head + 1] - a_cs_t[head:head + 1, :]
                decay = jnp.exp(jnp.where(causal, seg, NEG_BIG))
                w = (cb * decay).astype(BF16)
                xh = jnp.where(lo_half if hh == 0 else ~lo_half, xd_p, zeros_b)
                d = jnp.dot(w, xh, preferred_element_type=F32)
                y_p = d if y_p is None else y_p + d
            st = state_ref[pair]
            y_off = jnp.dot(c_g, st.astype(BF16), preferred_element_type=F32)
            y_parts.append(y_p + y_off * grow_x[:, lanes])
            new = lax.dot_general(b_g, xdt_b[:, lanes], (((0,), (0,)), ((), ())),
                                  preferred_element_type=F32)
            state_ref[pair] = st * end_x[:, lanes] + new

    y = jnp.concatenate(y_parts, axis=-1) + xs * dskip_ref[...]
    gated = y * _silu(z_ref[...].astype(F32))
    ms = jnp.mean(gated * gated, axis=-1, keepdims=True)
    o_ref[...] = (gated * lax.rsqrt(ms + EPS) * nw_ref[...]).astype(o_ref.dtype)


def _ssd_mixer(proj, dt_raw, conv_w, conv_b, dt_bias, a_log, d_skip_x, norm_w):
    s = proj.shape[0]
    L = SSD_CHUNK
    full = lambda shape: pl.BlockSpec(shape, lambda c: (0, 0))
    return pl.pallas_call(
        _ssd_kernel,
        grid=(s // L,),
        in_specs=[
            pl.BlockSpec((L, D_SSM), lambda c: (c, 2)),
            pl.BlockSpec((L, D_SSM), lambda c: (c, 3)),
            pl.BlockSpec((L, D_SSM), lambda c: (c, 4)),
            pl.BlockSpec((L, LANES), lambda c: (c, 0)),
            full((CONV_WIDTH, D_XBC)),
            full((1, D_XBC)),
            full((1, LANES)),
            full((1, LANES)),
            full((1, D_SSM)),
            full((1, D_SSM)),
        ],
        out_specs=pl.BlockSpec((L, D_SSM), lambda c: (c, 0)),
        out_shape=jax.ShapeDtypeStruct((s, D_SSM), BF16),
        scratch_shapes=[pltpu.VMEM((L + 8, D_XBC), F32),
                        pltpu.VMEM((N_SSM_HEADS // 2, SSM_STATE, LANES), F32)],
        compiler_params=_params("arbitrary"),
        name="ssd_mixer",
    )(proj, proj, proj, dt_raw, conv_w, conv_b, dt_bias, a_log, d_skip_x, norm_w)


def _out_proj_kernel(a_ref, b_ref, wa_ref, wb_ref, x_ref, o_ref):
    acc = jnp.dot(a_ref[...], wa_ref[...], preferred_element_type=F32)
    acc = acc + jnp.dot(b_ref[...], wb_ref[...], preferred_element_type=F32)
    o_ref[...] = x_ref[...] + acc


def _out_proj(a, b, w, x, tm, tn):
    s = x.shape[0]
    ka, kb = a.shape[1], b.shape[1]
    return pl.pallas_call(
        _out_proj_kernel,
        grid=(s // tm, D_MODEL // tn),
        in_specs=[
            pl.BlockSpec((tm, ka), lambda i, j: (i, 0)),
            pl.BlockSpec((tm, kb), lambda i, j: (i, 0)),
            pl.BlockSpec((ka, tn), lambda i, j: (0, j)),
            pl.BlockSpec((kb, tn), lambda i, j: (1, j)),
            pl.BlockSpec((tm, tn), lambda i, j: (i, j)),
        ],
        out_specs=pl.BlockSpec((tm, tn), lambda i, j: (i, j)),
        out_shape=jax.ShapeDtypeStruct((s, D_MODEL), F32),
        compiler_params=_params("parallel", "arbitrary"),
        name="out_proj",
    )(a, b, w, w, x)


def _ffn_kernel(x_ref, nw_ref, wg_ref, wu_ref, wd_ref, fw_ref, o_ref, h_ref, *, final_norm):
    f = pl.program_id(1)

    @pl.when(f == 0)
    def _():
        xf = x_ref[...]
        ms = jnp.mean(xf * xf, axis=-1, keepdims=True)
        h_ref[...] = (xf * lax.rsqrt(ms + EPS) * nw_ref[...]).astype(BF16)
        o_ref[...] = xf

    h = h_ref[...]
    g = jnp.dot(h, wg_ref[...], preferred_element_type=F32)
    u = jnp.dot(h, wu_ref[...], preferred_element_type=F32)
    act = (_silu(g) * u).astype(BF16)
    o_ref[...] += jnp.dot(act, wd_ref[...], preferred_element_type=F32)

    if final_norm:
        @pl.when(f == pl.num_programs(1) - 1)
        def _():
            xf = o_ref[...]
            ms = jnp.mean(xf * xf, axis=-1, keepdims=True)
            o_ref[...] = xf * lax.rsqrt(ms + EPS) * fw_ref[...]


def _ffn(x, nw, w_gu, w_d, final_w, final_norm, tm, tf):
    s, d = x.shape
    nf = D_FF // tf
    kern = functools.partial(_ffn_kernel, final_norm=final_norm)
    return pl.pallas_call(
        kern,
        grid=(s // tm, nf),
        in_specs=[
            pl.BlockSpec((tm, d), lambda i, f: (i, 0)),
            pl.BlockSpec((1, d), lambda i, f: (0, 0)),
            pl.BlockSpec((d, tf), lambda i, f: (0, f)),
            pl.BlockSpec((d, tf), lambda i, f: (0, nf + f)),
            pl.BlockSpec((tf, d), lambda i, f: (f, 0)),
            pl.BlockSpec((1, d), lambda i, f: (0, 0)),
        ],
        out_specs=pl.BlockSpec((tm, d), lambda i, f: (i, 0)),
        out_shape=jax.ShapeDtypeStruct((s, d), F32),
        scratch_shapes=[pltpu.VMEM((tm, d), BF16)],
        compiler_params=_params("parallel", "arbitrary"),
        name="ffn",
    )(x, nw, w_gu, w_gu, w_d, final_w)


def _alibi_slopes(n_heads):
    start = 2.0 ** (-8.0 / n_heads)
    return start ** jnp.arange(1, n_heads + 1, dtype=F32)


def kernel(x, norm_mix_w, w_in, diff_lambda, subln_w, conv_w, conv_b, dt_bias, a_log,
           d_skip, ssm_norm_w, w_out, norm_ffn_w, w_gate_up, w_down, norm_final_w):
    b, s_len, _ = x.shape
    assert b == 1
    depth = w_in.shape[0]
    o_v, o_z = 2 * D_ATTN, 3 * D_ATTN
    o_dt = o_z + D_SSM + D_XBC
    n_dt = w_in.shape[2] - o_dt
    tm = min(512, s_len)
    tm_big = min(1024, s_len)
    tq = min(512, s_len)

    slopes = _alibi_slopes(N_ATTN_HEADS)
    q_scale = ATTN_HEAD_DIM ** -0.5 * LOG2E
    colscale = jnp.concatenate([jnp.full((1, D_ATTN), q_scale, F32),
                                jnp.ones((1, o_dt - 2 * D_ATTN), F32)], axis=1)
    ones_dt = jnp.ones((1, LANES), F32)
    pad_l = lambda v: jnp.pad(v.astype(F32), (0, LANES - v.shape[0]))[None, :]

    xc = x[0]
    for l in range(depth):
        w_main = jnp.concatenate([w_in[l, :, :o_v], w_in[l, :, o_z:o_dt]], axis=1).astype(BF16)
        w_vt = w_in[l, :, o_v:o_z].T.astype(BF16)
        w_dt = jnp.pad(w_in[l, :, o_dt:], ((0, 0), (0, LANES - n_dt))).astype(BF16)
        nw = norm_mix_w[l][None, :]
        proj = _rms_matmul(xc, nw, w_main, colscale, BF16, tm_big, 1024)
        vt = _rms_matmul_t(xc, nw, w_vt, BF16, tm)
        dt_raw = _rms_matmul(xc, nw, w_dt, ones_dt, F32, tm_big, LANES)

        lam_init = 0.8 - 0.6 * math.exp(-0.3 * l)
        attn_out = _diff_attention(proj, vt, slopes, diff_lambda[l], subln_w[l][:, None],
                                   lam_init, tq, ATTN_HEADS_PER_STEP)
        ssm_out = _ssd_mixer(proj, dt_raw, conv_w[l], conv_b[l][None, :], pad_l(dt_bias[l]),
                             pad_l(a_log[l]), jnp.repeat(d_skip[l], SSM_HEAD_DIM)[None, :],
                             ssm_norm_w[l][None, :])
        xc = _out_proj(attn_out, ssm_out, w_out[l].astype(BF16), xc, tm, D_MODEL)
        xc = _ffn(xc, norm_ffn_w[l][None, :], w_gate_up[l].astype(BF16),
                  w_down[l].astype(BF16), norm_final_w[None, :], l == depth - 1, tm_big, 512)
    return xc[None]
```

```python
import functools
import math

import jax
import jax.numpy as jnp
from jax import lax
from jax.experimental import pallas as pl
from jax.experimental.pallas import tpu as pltpu

D_MODEL = 2048
CHUNK = 64
D_ATTN = D_MODEL // 2
D_SSM = D_MODEL - D_ATTN
ATTN_HEAD_DIM = 64
ATTN_VDIM = 2 * ATTN_HEAD_DIM
N_ATTN_HEADS = D_ATTN // ATTN_VDIM
SSM_HEAD_DIM = 64
N_SSM_HEADS = D_SSM // SSM_HEAD_DIM
SSM_STATE = 128
SSM_GROUPS = 4
CONV_WIDTH = 4
SSD_CHUNK = 128
D_XBC = D_SSM + 2 * SSM_GROUPS * SSM_STATE
D_FF = ((8 * D_MODEL // 3 + 255) // 256) * 256
EPS = 1e-5

LANES = 128
LOG2E = math.log2(math.e)
NEG_BIG = -1e30
VMEM_LIMIT = 56 * 1024 * 1024
ATTN_HEADS_PER_STEP = 2

F32 = jnp.float32
BF16 = jnp.bfloat16


def _params(*sem):
    return pltpu.CompilerParams(dimension_semantics=sem, vmem_limit_bytes=VMEM_LIMIT)


def _rms_matmul_kernel(x_ref, nw_ref, w_ref, cs_ref, o_ref, h_ref):
    @pl.when(pl.program_id(1) == 0)
    def _():
        xf = x_ref[...]
        ms = jnp.mean(xf * xf, axis=-1, keepdims=True)
        h_ref[...] = (xf * lax.rsqrt(ms + EPS) * nw_ref[...]).astype(BF16)

    acc = jnp.dot(h_ref[...], w_ref[...], preferred_element_type=F32)
    o_ref[...] = (acc * cs_ref[...]).astype(o_ref.dtype)


def _rms_matmul(x, nw, w, colscale, out_dtype, tm, tn):
    s, d = x.shape
    n = w.shape[1]
    return pl.pallas_call(
        _rms_matmul_kernel,
        grid=(s // tm, n // tn),
        in_specs=[
            pl.BlockSpec((tm, d), lambda i, j: (i, 0)),
            pl.BlockSpec((1, d), lambda i, j: (0, 0)),
            pl.BlockSpec((d, tn), lambda i, j: (0, j)),
            pl.BlockSpec((1, tn), lambda i, j: (0, j)),
        ],
        out_specs=pl.BlockSpec((tm, tn), lambda i, j: (i, j)),
        out_shape=jax.ShapeDtypeStruct((s, n), out_dtype),
        scratch_shapes=[pltpu.VMEM((tm, d), BF16)],
        compiler_params=_params("parallel", "arbitrary"),
        name="rms_matmul",
    )(x, nw, w, colscale)


def _kv_dt_kernel(x_ref, nw_ref, wk_ref, wvt_ref, wdt_ref, k_ref, vt_ref, dt_ref):
    xf = x_ref[...]
    ms = jnp.mean(xf * xf, axis=-1, keepdims=True)
    h = (xf * lax.rsqrt(ms + EPS) * nw_ref[...]).astype(BF16)
    k = jnp.dot(h, wk_ref[...], preferred_element_type=F32)
    gw = k_ref.shape[2]
    for grp in range(k_ref.shape[0]):
        k_ref[grp] = k[:, grp * gw:(grp + 1) * gw].astype(k_ref.dtype)
    vt_ref[...] = lax.dot_general(wvt_ref[...], h, (((1,), (1,)), ((), ())),
                                  preferred_element_type=F32).astype(vt_ref.dtype)
    dt_ref[...] = jnp.dot(h, wdt_ref[...], preferred_element_type=F32)


def _kv_dt_proj(x, nw, wk, wvt, wdt, gw, tm):
    s, d = x.shape
    nk, nv, ndt = wk.shape[1], wvt.shape[0], wdt.shape[1]
    full = lambda shape: pl.BlockSpec(shape, lambda i: (0, 0))
    return pl.pallas_call(
        _kv_dt_kernel,
        grid=(s // tm,),
        in_specs=[
            pl.BlockSpec((tm, d), lambda i: (i, 0)),
            full((1, d)),
            full((d, nk)),
            full((nv, d)),
            full((d, ndt)),
        ],
        out_specs=[
            pl.BlockSpec((nk // gw, tm, gw), lambda i: (0, i, 0)),
            pl.BlockSpec((nv, tm), lambda i: (0, i)),
            pl.BlockSpec((tm, ndt), lambda i: (i, 0)),
        ],
        out_shape=[
            jax.ShapeDtypeStruct((nk // gw, s, gw), BF16),
            jax.ShapeDtypeStruct((nv, s), BF16),
            jax.ShapeDtypeStruct((s, ndt), F32),
        ],
        compiler_params=_params("parallel"),
        name="kv_dt_proj",
    )(x, nw, wk, wvt, wdt)


ONES_ROWS = 16
ACC_ROWS = ATTN_VDIM + ONES_ROWS


def _split3(a):
    hi = a.astype(BF16)
    r1 = a - hi.astype(F32)
    mid = r1.astype(BF16)
    lo = (r1 - mid.astype(F32)).astype(BF16)
    return hi, mid, lo


def _attn_kernel(slopes_ref, dl_ref, q_ref, k_ref, vt_ref, sw_ref, o_ref,
                 acc_ref, s_ref, p_ref, *, tq, hps, lam_init):
    g = pl.program_id(0)
    i = pl.program_id(1)
    heads = range(hps)
    hm = [(hh, j) for hh in heads for j in range(2)]
    cs = [slopes_ref[g * hps + hh] * LOG2E for hh in heads]

    shape = (tq, LANES)
    lane = lax.broadcasted_iota(jnp.int32, shape, 1)
    zero = jnp.zeros(shape, BF16)
    krow = lax.broadcasted_iota(jnp.int32, shape, 0)
    k_split = jnp.where((lane & 1) == 0, (krow >> 4) << 4, krow & 15)
    k_bias = jnp.where(lane < 6, k_split, 0).astype(F32).astype(BF16)
    q_maps = {}
    for hh in heads:
        q = q_ref[:, hh * LANES:(hh + 1) * LANES]
        c_terms = _split3(jnp.full(shape, cs[hh], F32))
        q_bias = jnp.zeros(shape, F32)
        for t in range(3):
            q_bias = jnp.where((lane >> 1) == t, c_terms[t].astype(F32), q_bias)
        q_bias = q_bias.astype(BF16)
        q_maps[hh, 0] = jnp.concatenate([jnp.where(lane < ATTN_HEAD_DIM, q, zero), q_bias], axis=1)
        q_maps[hh, 1] = jnp.concatenate([jnp.where(lane >= ATTN_HEAD_DIM, q, zero), q_bias], axis=1)

    acc_ref[...] = jnp.zeros_like(acc_ref)
    p_ref[1] = jnp.zeros(p_ref.shape[1:], BF16)
    ones_rows = jnp.ones((ONES_ROWS, tq), BF16)

    def scores(blk, slot):
        k0 = pl.multiple_of(blk * tq, tq)
        col_max = {}
        for hh in heads:
            kblk = jnp.concatenate([k_ref[pl.ds(k0, tq), hh * LANES:(hh + 1) * LANES], k_bias],
                                   axis=1)
            for j in range(2):
                s = lax.dot_general(kblk, q_maps[hh, j], (((1,), (1,)), ((), ())),
                                    preferred_element_type=F32)
                s_ref[slot, hh, j] = s
                col_max[hh, j] = jnp.max(s, axis=0, keepdims=True)
        return col_max

    def softmax(slot, bias, offs, ms, col_max):
        new_ms, alphas = {}, {}
        for hh, j in hm:
            s = s_ref[slot, hh, j]
            if bias is None:
                s_max = col_max[hh, j]
            else:
                s = s + bias[hh]
                s_max = jnp.max(s, axis=0, keepdims=True)
            m_new = jnp.maximum(ms[hh, j], s_max + offs[hh])
            alphas[hh, j] = jnp.exp2(ms[hh, j] - m_new)
            p_ref[slot, hh, j] = jnp.exp2(s - (m_new - offs[hh])).astype(BF16)
            new_ms[hh, j] = m_new
        return new_ms, alphas

    def values(blk, slot, alphas):
        k0 = pl.multiple_of(blk * tq, tq)
        for hh in heads:
            lhs = jnp.concatenate([vt_ref[hh * ATTN_VDIM:(hh + 1) * ATTN_VDIM, pl.ds(k0, tq)],
                                   ones_rows], axis=0)
            for j in range(2):
                pv = jnp.dot(lhs, p_ref[slot, hh, j], preferred_element_type=F32)
                acc_ref[hh, j] = alphas[hh, j] * acc_ref[hh, j] + pv

    init_m = {k: jnp.full((1, tq), NEG_BIG, F32) for k in hm}
    init_a = {k: jnp.ones((1, tq), F32) for k in hm}
    col_max0 = scores(0, 0)

    def past_block(t, slot, carry):
        ms, alphas, col_max = carry
        values(jnp.maximum(t - 1, 0), 1 - slot, alphas)
        dist = ((i - t) * tq).astype(F32)
        ms, alphas = softmax(slot, None, [-cs[hh] * dist for hh in heads], ms, col_max)
        col_max = scores(t + 1, 1 - slot)
        return ms, alphas, col_max

    def past_step(t, carry):
        return lax.cond((t & 1) == 0, functools.partial(past_block, t, 0),
                        functools.partial(past_block, t, 1), carry)

    carry = lax.fori_loop(0, i, past_step, (init_m, init_a, col_max0))

    dl = dl_ref[...]
    lam = (jnp.exp(jnp.sum(dl[0:1] * dl[1:2], axis=-1, keepdims=True))
           - jnp.exp(jnp.sum(dl[2:3] * dl[3:4], axis=-1, keepdims=True)) + lam_init)

    def finish(slot):
        ms, alphas, _ = carry
        values(jnp.maximum(i - 1, 0), 1 - slot, alphas)
        kl = lax.broadcasted_iota(jnp.int32, (tq, tq), 0)
        ql = lax.broadcasted_iota(jnp.int32, (tq, tq), 1)
        allowed = (kl // CHUNK) <= (ql // CHUNK)
        ahead = jnp.maximum(kl - ql, 0).astype(F32)
        bias = [jnp.where(allowed, (-2.0 * cs[hh]) * ahead, NEG_BIG) for hh in heads]
        ms, alphas = softmax(slot, bias, [0.0] * hps, ms, None)
        values(i, slot, alphas)
        for hh in heads:
            a1, a2 = acc_ref[hh, 0], acc_ref[hh, 1]
            o = (a1[:ATTN_VDIM] * (1.0 / a1[ATTN_VDIM:ATTN_VDIM + 1])
                 - lam * (a2[:ATTN_VDIM] * (1.0 / a2[ATTN_VDIM:ATTN_VDIM + 1])))
            ms_o = jnp.mean(o * o, axis=0, keepdims=True)
            y = (o * lax.rsqrt(ms_o + EPS) * sw_ref[...]) * (1.0 - lam_init)
            o_ref[:, hh * LANES:(hh + 1) * LANES] = y.T.astype(o_ref.dtype)

    for parity in range(2):
        pl.when((i & 1) == parity)(functools.partial(finish, parity))


def _diff_attention(proj, k_groups, vt, slopes, dl, subw_col, lam_init, tq, hps):
    s = proj.shape[0]
    w = hps * LANES
    kern = functools.partial(_attn_kernel, tq=tq, hps=hps, lam_init=lam_init)
    return pl.pallas_call(
        kern,
        grid=(N_ATTN_HEADS // hps, s // tq),
        in_specs=[
            pl.BlockSpec(memory_space=pltpu.SMEM),
            pl.BlockSpec((4, ATTN_HEAD_DIM), lambda g, i: (0, 0)),
            pl.BlockSpec((tq, w), lambda g, i: (i, g)),
            pl.BlockSpec((None, s, w), lambda g, i: (g, 0, 0)),
            pl.BlockSpec((hps * ATTN_VDIM, s), lambda g, i: (g, 0)),
            pl.BlockSpec((ATTN_VDIM, 1), lambda g, i: (0, 0)),
        ],
        out_specs=pl.BlockSpec((tq, w), lambda g, i: (i, g)),
        out_shape=jax.ShapeDtypeStruct((s, D_ATTN), BF16),
        scratch_shapes=[pltpu.VMEM((hps, 2, ACC_ROWS, tq), F32),
                        pltpu.VMEM((2, hps, 2, tq, tq), F32),
                        pltpu.VMEM((2, hps, 2, tq, tq), BF16)],
        compiler_params=_params("parallel", "arbitrary"),
        name="diff_attention",
    )(slopes, dl, proj, k_groups, vt, subw_col)


def _dot3(a, b_bf16):
    out = None
    for t in _split3(a):
        d = jnp.dot(t, b_bf16, preferred_element_type=F32)
        out = d if out is None else out + d
    return out


def _silu(x):
    return x * (1.0 / (1.0 + jnp.exp(-x)))


def _ssd_kernel(z_ref, xr_ref, bcr_ref, dtr_ref, cw_ref, cb_ref, dtb_ref, alog_ref,
                dskip_ref, nw_ref, o_ref, ext_ref, state_ref):
    L = SSD_CHUNK
    c = pl.program_id(0)

    @pl.when(c == 0)
    def _():
        ext_ref[0:8, :] = jnp.zeros((8, D_XBC), F32)
        state_ref[...] = jnp.zeros_like(state_ref)

    ext_ref[8:8 + L, 0:D_SSM] = xr_ref[...].astype(F32)
    ext_ref[8:8 + L, D_SSM:D_XBC] = bcr_ref[...].astype(F32)
    conv = cb_ref[...]
    for t in range(CONV_WIDTH):
        conv = conv + cw_ref[t:t + 1, :] * ext_ref[5 + t:5 + t + L, :]
    ext_ref[0:8, :] = ext_ref[L:L + 8, :]
    xbc = _silu(conv)
    xs = xbc[:, :D_SSM]

    dt = dtr_ref[...] + dtb_ref[...]
    dt = jnp.maximum(dt, 0.0) + jnp.log1p(jnp.exp(-jnp.abs(dt)))
    a = -jnp.exp(alog_ref[...]) * dt
    row = lax.broadcasted_iota(jnp.int32, (L, L), 0)
    col = lax.broadcasted_iota(jnp.int32, (L, L), 1)
    causal = col <= row
    tril = jnp.where(causal, 1.0, 0.0).astype(BF16)
    a_cs = None
    for t in _split3(a):
        d = jnp.dot(tril, t, preferred_element_type=F32)
        a_cs = d if a_cs is None else a_cs + d
    a_cs_t = a_cs.T
    a_end = a_cs[L - 1:L, :]

    hrow = lax.broadcasted_iota(jnp.int32, (LANES, D_SSM), 0)
    ccol = lax.broadcasted_iota(jnp.int32, (LANES, D_SSM), 1)
    expand = jnp.where(ccol // SSM_HEAD_DIM == hrow, 1.0, 0.0).astype(BF16)
    dt_x = _dot3(dt, expand)
    grow_x = _dot3(jnp.exp(a_cs), expand)
    tail_x = _dot3(jnp.exp(a_end - a_cs), expand)
    end_x = grow_x[L - 1:L, :]

    xd = xs * dt_x
    xd_b = xd.astype(BF16)
    xdt_b = (xd * tail_x).astype(BF16)
    lane = lax.broadcasted_iota(jnp.int32, (L, LANES), 1)
    lo_half = lane < SSM_HEAD_DIM
    zeros_b = jnp.zeros((L, LANES), BF16)

    y_parts = []
    for g in range(SSM_GROUPS):
        b_g = xbc[:, D_SSM + g * SSM_STATE:D_SSM + (g + 1) * SSM_STATE].astype(BF16)
        c_g = xbc[:, D_SSM + (SSM_GROUPS + g) * SSM_STATE:
                  D_SSM + (SSM_GROUPS + g + 1) * SSM_STATE].astype(BF16)
        cb = lax.dot_general(c_g, b_g, (((1,), (1,)), ((), ())),
                             preferred_element_type=F32)
        for pr in range(2):
            pair = 2 * g + pr
            lanes = slice(pair * LANES, (pair + 1) * LANES)
            xd_p = xd_b[:, lanes]
            y_p = None
            for hh in range(2):
                head = 2 * pair + hh
                seg = a_cs[:, head:head + 1] - a_cs_t[head:head + 1, :]
                decay = jnp.exp(jnp.where(causal, seg, NEG_BIG))
                w = (cb * decay).astype(BF16)
                xh = jnp.where(lo_half if hh == 0 else ~lo_half, xd_p, zeros_b)
                d = jnp.dot(w, xh, preferred_element_type=F32)
                y_p = d if y_p is None else y_p + d
            st = state_ref[pair]
            y_off = jnp.dot(c_g, st.astype(BF16), preferred_element_type=F32)
            y_parts.append(y_p + y_off * grow_x[:, lanes])
            new = lax.dot_general(b_g, xdt_b[:, lanes], (((0,), (0,)), ((), ())),
                                  preferred_element_type=F32)
            state_ref[pair] = st * end_x[:, lanes] + new

    y = jnp.concatenate(y_parts, axis=-1) + xs * dskip_ref[...]
    gated = y * _silu(z_ref[...].astype(F32))
    ms = jnp.mean(gated * gated, axis=-1, keepdims=True)
    o_ref[...] = (gated * lax.rsqrt(ms + EPS) * nw_ref[...]).astype(o_ref.dtype)


def _ssd_mixer(proj, dt_raw, conv_w, conv_b, dt_bias, a_log, d_skip_x, norm_w):
    s = proj.shape[0]
    L = SSD_CHUNK
    full = lambda shape: pl.BlockSpec(shape, lambda c: (0, 0))
    return pl.pallas_call(
        _ssd_kernel,
        grid=(s // L,),
        in_specs=[
            pl.BlockSpec((L, D_SSM), lambda c: (c, 1)),
            pl.BlockSpec((L, D_SSM), lambda c: (c, 2)),
            pl.BlockSpec((L, D_SSM), lambda c: (c, 3)),
            pl.BlockSpec((L, LANES), lambda c: (c, 0)),
            full((CONV_WIDTH, D_XBC)),
            full((1, D_XBC)),
            full((1, LANES)),
            full((1, LANES)),
            full((1, D_SSM)),
            full((1, D_SSM)),
        ],
        out_specs=pl.BlockSpec((L, D_SSM), lambda c: (c, 0)),
        out_shape=jax.ShapeDtypeStruct((s, D_SSM), BF16),
        scratch_shapes=[pltpu.VMEM((L + 8, D_XBC), F32),
                        pltpu.VMEM((N_SSM_HEADS // 2, SSM_STATE, LANES), F32)],
        compiler_params=_params("arbitrary"),
        name="ssd_mixer",
    )(proj, proj, proj, dt_raw, conv_w, conv_b, dt_bias, a_log, d_skip_x, norm_w)


def _out_proj_kernel(a_ref, b_ref, wa_ref, wb_ref, x_ref, o_ref):
    acc = jnp.dot(a_ref[...], wa_ref[...], preferred_element_type=F32)
    acc = acc + jnp.dot(b_ref[...], wb_ref[...], preferred_element_type=F32)
    o_ref[...] = x_ref[...] + acc


def _out_proj(a, b, w, x, tm, tn):
    s = x.shape[0]
    ka, kb = a.shape[1], b.shape[1]
    return pl.pallas_call(
        _out_proj_kernel,
        grid=(s // tm, D_MODEL // tn),
        in_specs=[
            pl.BlockSpec((tm, ka), lambda i, j: (i, 0)),
            pl.BlockSpec((tm, kb), lambda i, j: (i, 0)),
            pl.BlockSpec((ka, tn), lambda i, j: (0, j)),
            pl.BlockSpec((kb, tn), lambda i, j: (1, j)),
            pl.BlockSpec((tm, tn), lambda i, j: (i, j)),
        ],
        out_specs=pl.BlockSpec((tm, tn), lambda i, j: (i, j)),
        out_shape=jax.ShapeDtypeStruct((s, D_MODEL), F32),
        compiler_params=_params("parallel", "arbitrary"),
        name="out_proj",
    )(a, b, w, w, x)


def _ffn_kernel(x_ref, nw_ref, wg_ref, wu_ref, wd_ref, fw_ref, o_ref, h_ref, *, final_norm):
    f = pl.program_id(1)

    @pl.when(f == 0)
    def _():
        xf = x_ref[...]
        ms = jnp.mean(xf * xf, axis=-1, keepdims=True)
        h_ref[...] = (xf * lax.rsqrt(ms + EPS) * nw_ref[...]).astype(BF16)
        o_ref[...] = xf

    h = h_ref[...]
    g = jnp.dot(h, wg_ref[...], preferred_element_type=F32)
    u = jnp.dot(h, wu_ref[...], preferred_element_type=F32)
    act = (_silu(g) * u).astype(BF16)
    o_ref[...] += jnp.dot(act, wd_ref[...], preferred_element_type=F32)

    if final_norm:
        @pl.when(f == pl.num_programs(1) - 1)
        def _():
            xf = o_ref[...]
            ms = jnp.mean(xf * xf, axis=-1, keepdims=True)
            o_ref[...] = xf * lax.rsqrt(ms + EPS) * fw_ref[...]


def _ffn(x, nw, w_gu, w_d, final_w, final_norm, tm, tf):
    s, d = x.shape
    nf = D_FF // tf
    kern = functools.partial(_ffn_kernel, final_norm=final_norm)
    return pl.pallas_call(
        kern,
        grid=(s // tm, nf),
        in_specs=[
            pl.BlockSpec((tm, d), lambda i, f: (i, 0)),
            pl.BlockSpec((1, d), lambda i, f: (0, 0)),
            pl.BlockSpec((d, tf), lambda i, f: (0, f)),
            pl.BlockSpec((d, tf), lambda i, f: (0, nf + f)),
            pl.BlockSpec((tf, d), lambda i, f: (f, 0)),
            pl.BlockSpec((1, d), lambda i, f: (0, 0)),
        ],
        out_specs=pl.BlockSpec((tm, d), lambda i, f: (i, 0)),
        out_shape=jax.ShapeDtypeStruct((s, d), F32),
        scratch_shapes=[pltpu.VMEM((tm, d), BF16)],
        compiler_params=_params("parallel", "arbitrary"),
        name="ffn",
    )(x, nw, w_gu, w_gu, w_d, final_w)


def _alibi_slopes(n_heads):
    start = 2.0 ** (-8.0 / n_heads)
    return start ** jnp.arange(1, n_heads + 1, dtype=F32)


def kernel(x, norm_mix_w, w_in, diff_lambda, subln_w, conv_w, conv_b, dt_bias, a_log,
           d_skip, ssm_norm_w, w_out, norm_ffn_w, w_gate_up, w_down, norm_final_w):
    b, s_len, _ = x.shape
    assert b == 1
    depth = w_in.shape[0]
    o_k, o_v, o_z = D_ATTN, 2 * D_ATTN, 3 * D_ATTN
    o_dt = o_z + D_SSM + D_XBC
    n_dt = w_in.shape[2] - o_dt
    tm = min(512, s_len)
    tm_big = min(1024, s_len)
    tq = min(512, s_len)
    hps = ATTN_HEADS_PER_STEP

    slopes = _alibi_slopes(N_ATTN_HEADS)
    q_scale = ATTN_HEAD_DIM ** -0.5 * LOG2E
    colscale = jnp.concatenate([jnp.full((1, D_ATTN), q_scale, F32),
                                jnp.ones((1, o_dt - o_z), F32)], axis=1)
    pad_l = lambda v: jnp.pad(v.astype(F32), (0, LANES - v.shape[0]))[None, :]
    w_in_b = w_in.astype(BF16)

    xc = x[0]
    for l in range(depth):
        w_main = jnp.concatenate([w_in_b[l, :, :o_k], w_in_b[l, :, o_z:o_dt]], axis=1)
        w_k = w_in_b[l, :, o_k:o_v]
        w_vt = w_in_b[l, :, o_v:o_z].T
        w_dt = jnp.pad(w_in_b[l, :, o_dt:], ((0, 0), (0, LANES - n_dt)))
        nw = norm_mix_w[l][None, :]
        proj = _rms_matmul(xc, nw, w_main, colscale, BF16, tm_big, 1024)
        k_groups, vt, dt_raw = _kv_dt_proj(xc, nw, w_k, w_vt, w_dt, hps * LANES, tm)

        lam_init = 0.8 - 0.6 * math.exp(-0.3 * l)
        attn_out = _diff_attention(proj, k_groups, vt, slopes, diff_lambda[l], subln_w[l][:, None],
                                   lam_init, tq, hps)
        ssm_out = _ssd_mixer(proj, dt_raw, conv_w[l], conv_b[l][None, :], pad_l(dt_bias[l]),
                             pad_l(a_log[l]), jnp.repeat(d_skip[l], SSM_HEAD_DIM)[None, :],
                             ssm_norm_w[l][None, :])
        xc = _out_proj(attn_out, ssm_out, w_out[l].astype(BF16), xc, tm, D_MODEL)
        xc = _ffn(xc, norm_ffn_w[l][None, :], w_gate_up[l].astype(BF16),
                  w_down[l].astype(BF16), norm_final_w[None, :], l == depth - 1, tm_big, 512)
    return xc[None]
```

```python
import functools
import math

import jax
import jax.numpy as jnp
from jax import lax
from jax.experimental import pallas as pl
from jax.experimental.pallas import tpu as pltpu

D_MODEL = 2048
CHUNK = 64
D_ATTN = D_MODEL // 2
D_SSM = D_MODEL - D_ATTN
ATTN_HEAD_DIM = 64
ATTN_VDIM = 2 * ATTN_HEAD_DIM
N_ATTN_HEADS = D_ATTN // ATTN_VDIM
SSM_HEAD_DIM = 64
N_SSM_HEADS = D_SSM // SSM_HEAD_DIM
SSM_STATE = 128
SSM_GROUPS = 4
CONV_WIDTH = 4
SSD_CHUNK = 128
D_XBC = D_SSM + 2 * SSM_GROUPS * SSM_STATE
D_FF = ((8 * D_MODEL // 3 + 255) // 256) * 256
EPS = 1e-5

LANES = 128
LOG2E = math.log2(math.e)
NEG_BIG = -1e30
VMEM_LIMIT = 56 * 1024 * 1024
ATTN_HEADS_PER_STEP = 2

F32 = jnp.float32
BF16 = jnp.bfloat16


def _params(*sem):
    return pltpu.CompilerParams(dimension_semantics=sem, vmem_limit_bytes=VMEM_LIMIT)


def _rms_matmul_kernel(x_ref, nw_ref, w_ref, cs_ref, o_ref, h_ref):
    @pl.when(pl.program_id(1) == 0)
    def _():
        xf = x_ref[...]
        ms = jnp.mean(xf * xf, axis=-1, keepdims=True)
        h_ref[...] = (xf * lax.rsqrt(ms + EPS) * nw_ref[...]).astype(BF16)

    acc = jnp.dot(h_ref[...], w_ref[...], preferred_element_type=F32)
    o_ref[...] = (acc * cs_ref[...]).astype(o_ref.dtype)


def _rms_matmul(x, nw, w, colscale, out_dtype, tm, tn):
    s, d = x.shape
    n = w.shape[1]
    return pl.pallas_call(
        _rms_matmul_kernel,
        grid=(s // tm, n // tn),
        in_specs=[
            pl.BlockSpec((tm, d), lambda i, j: (i, 0)),
            pl.BlockSpec((1, d), lambda i, j: (0, 0)),
            pl.BlockSpec((d, tn), lambda i, j: (0, j)),
            pl.BlockSpec((1, tn), lambda i, j: (0, j)),
        ],
        out_specs=pl.BlockSpec((tm, tn), lambda i, j: (i, j)),
        out_shape=jax.ShapeDtypeStruct((s, n), out_dtype),
        scratch_shapes=[pltpu.VMEM((tm, d), BF16)],
        compiler_params=_params("parallel", "arbitrary"),
        name="rms_matmul",
    )(x, nw, w, colscale)


def _kv_dt_kernel(x_ref, nw_ref, wk_ref, wvt_ref, wdt_ref, k_ref, vt_ref, dt_ref):
    xf = x_ref[...]
    ms = jnp.mean(xf * xf, axis=-1, keepdims=True)
    h = (xf * lax.rsqrt(ms + EPS) * nw_ref[...]).astype(BF16)
    k = jnp.dot(h, wk_ref[...], preferred_element_type=F32)
    gw = k_ref.shape[2]
    for grp in range(k_ref.shape[0]):
        k_ref[grp] = k[:, grp * gw:(grp + 1) * gw].astype(k_ref.dtype)
    vt_ref[...] = lax.dot_general(wvt_ref[...], h, (((1,), (1,)), ((), ())),
                                  preferred_element_type=F32).astype(vt_ref.dtype)
    dt_ref[...] = jnp.dot(h, wdt_ref[...], preferred_element_type=F32)


def _kv_dt_proj(x, nw, wk, wvt, wdt, gw, tm):
    s, d = x.shape
    nk, nv, ndt = wk.shape[1], wvt.shape[0], wdt.shape[1]
    full = lambda shape: pl.BlockSpec(shape, lambda i: (0, 0))
    return pl.pallas_call(
        _kv_dt_kernel,
        grid=(s // tm,),
        in_specs=[
            pl.BlockSpec((tm, d), lambda i: (i, 0)),
            full((1, d)),
            full((d, nk)),
            full((nv, d)),
            full((d, ndt)),
        ],
        out_specs=[
            pl.BlockSpec((nk // gw, tm, gw), lambda i: (0, i, 0)),
            pl.BlockSpec((nv, tm), lambda i: (0, i)),
            pl.BlockSpec((tm, ndt), lambda i: (i, 0)),
        ],
        out_shape=[
            jax.ShapeDtypeStruct((nk // gw, s, gw), BF16),
            jax.ShapeDtypeStruct((nv, s), BF16),
            jax.ShapeDtypeStruct((s, ndt), F32),
        ],
        compiler_params=_params("parallel"),
        name="kv_dt_proj",
    )(x, nw, wk, wvt, wdt)


ONES_ROWS = 16
ACC_ROWS = ATTN_VDIM + ONES_ROWS


def _split3(a):
    hi = a.astype(BF16)
    r1 = a - hi.astype(F32)
    mid = r1.astype(BF16)
    lo = (r1 - mid.astype(F32)).astype(BF16)
    return hi, mid, lo


def _attn_kernel(slopes_ref, dl_ref, q_ref, k_ref, vt_ref, sw_ref, o_ref,
                 acc_ref, s_ref, p_ref, *, tq, hps, lam_init):
    g = pl.program_id(0)
    i = pl.program_id(1)
    heads = range(hps)
    hm = [(hh, j) for hh in heads for j in range(2)]
    cs = [slopes_ref[g * hps + hh] * LOG2E for hh in heads]

    shape = (tq, LANES)
    lane = lax.broadcasted_iota(jnp.int32, shape, 1)
    zero = jnp.zeros(shape, BF16)
    krow = lax.broadcasted_iota(jnp.int32, shape, 0)
    k_split = jnp.where((lane & 1) == 0, (krow >> 4) << 4, krow & 15)
    k_bias = jnp.where(lane < 6, k_split, 0).astype(F32).astype(BF16)
    q_maps = {}
    for hh in heads:
        q = q_ref[:, hh * LANES:(hh + 1) * LANES]
        c_terms = _split3(jnp.full(shape, cs[hh], F32))
        q_bias = jnp.zeros(shape, F32)
        for t in range(3):
            q_bias = jnp.where((lane >> 1) == t, c_terms[t].astype(F32), q_bias)
        q_bias = q_bias.astype(BF16)
        q_maps[hh, 0] = jnp.concatenate([jnp.where(lane < ATTN_HEAD_DIM, q, zero), q_bias], axis=1)
        q_maps[hh, 1] = jnp.concatenate([jnp.where(lane >= ATTN_HEAD_DIM, q, zero), q_bias], axis=1)

    acc_ref[...] = jnp.zeros_like(acc_ref)
    ones_rows = jnp.ones((ONES_ROWS, tq), BF16)

    def scores(blk, slot):
        k0 = pl.multiple_of(blk * tq, tq)
        col_max = {}
        for hh in heads:
            kblk = jnp.concatenate([k_ref[pl.ds(k0, tq), hh * LANES:(hh + 1) * LANES], k_bias],
                                   axis=1)
            for j in range(2):
                s = lax.dot_general(kblk, q_maps[hh, j], (((1,), (1,)), ((), ())),
                                    preferred_element_type=F32)
                s_ref[slot, hh, j] = s
                col_max[hh, j] = jnp.max(s, axis=0, keepdims=True)
        return col_max

    def softmax(slot, bias, offs, ms, col_max):
        new_ms, alphas = {}, {}
        for hh, j in hm:
            s = s_ref[slot, hh, j]
            if bias is None:
                s_max = col_max[hh, j]
            else:
                s = s + bias[hh]
                s_max = jnp.max(s, axis=0, keepdims=True)
            m_new = jnp.maximum(ms[hh, j], s_max + offs[hh])
            alphas[hh, j] = jnp.exp2(ms[hh, j] - m_new)
            p_ref[slot, hh, j] = jnp.exp2(s - (m_new - offs[hh])).astype(BF16)
            new_ms[hh, j] = m_new
        return new_ms, alphas

    def values(blk, slot, alphas):
        k0 = pl.multiple_of(blk * tq, tq)
        for hh in heads:
            lhs = jnp.concatenate([vt_ref[hh * ATTN_VDIM:(hh + 1) * ATTN_VDIM, pl.ds(k0, tq)],
                                   ones_rows], axis=0)
            for j in range(2):
                pv = jnp.dot(lhs, p_ref[slot, hh, j], preferred_element_type=F32)
                acc_ref[hh, j] = alphas[hh, j] * acc_ref[hh, j] + pv

    init_m = {k: jnp.full((1, tq), NEG_BIG, F32) for k in hm}
    init_a = {k: jnp.ones((1, tq), F32) for k in hm}
    col_max0 = scores(0, 0)

    def past_block(t, slot, has_prev, carry):
        ms, prev_alphas, col_max = carry
        dist = ((i - t) * tq).astype(F32)
        ms, alphas = softmax(slot, None, [-cs[hh] * dist for hh in heads], ms, col_max)
        col_max = scores(t + 1, 1 - slot)
        if has_prev:
            values(t - 1, 1 - slot, prev_alphas)
        return ms, alphas, col_max

    def later_step(t, carry):
        return lax.cond((t & 1) == 0, functools.partial(past_block, t, 0, True),
                        functools.partial(past_block, t, 1, True), carry)

    def past_step(t, carry):
        return lax.cond(t == 0, functools.partial(past_block, t, 0, False),
                        functools.partial(later_step, t), carry)

    carry = lax.fori_loop(0, i, past_step, (init_m, init_a, col_max0))

    dl = dl_ref[...]
    lam = (jnp.exp(jnp.sum(dl[0:1] * dl[1:2], axis=-1, keepdims=True))
           - jnp.exp(jnp.sum(dl[2:3] * dl[3:4], axis=-1, keepdims=True)) + lam_init)

    def finish(slot, has_prev):
        ms, alphas, _ = carry
        if has_prev:
            values(i - 1, 1 - slot, alphas)
        kl = lax.broadcasted_iota(jnp.int32, (tq, tq), 0)
        ql = lax.broadcasted_iota(jnp.int32, (tq, tq), 1)
        allowed = (kl // CHUNK) <= (ql // CHUNK)
        ahead = jnp.maximum(kl - ql, 0).astype(F32)
        bias = [jnp.where(allowed, (-2.0 * cs[hh]) * ahead, NEG_BIG) for hh in heads]
        ms, alphas = softmax(slot, bias, [0.0] * hps, ms, None)
        values(i, slot, alphas)
        for hh in heads:
            a1, a2 = acc_ref[hh, 0], acc_ref[hh, 1]
            o = (a1[:ATTN_VDIM] * (1.0 / a1[ATTN_VDIM:ATTN_VDIM + 1])
                 - lam * (a2[:ATTN_VDIM] * (1.0 / a2[ATTN_VDIM:ATTN_VDIM + 1])))
            ms_o = jnp.mean(o * o, axis=0, keepdims=True)
            y = (o * lax.rsqrt(ms_o + EPS) * sw_ref[...]) * (1.0 - lam_init)
            o_ref[:, hh * LANES:(hh + 1) * LANES] = y.T.astype(o_ref.dtype)

    pl.when(i == 0)(functools.partial(finish, 0, False))
    pl.when((i > 0) & ((i & 1) == 0))(functools.partial(finish, 0, True))
    pl.when((i & 1) == 1)(functools.partial(finish, 1, True))


def _diff_attention(proj, k_groups, vt, slopes, dl, subw_col, lam_init, tq, hps):
    s = proj.shape[0]
    w = hps * LANES
    kern = functools.partial(_attn_kernel, tq=tq, hps=hps, lam_init=lam_init)
    return pl.pallas_call(
        kern,
        grid=(N_ATTN_HEADS // hps, s // tq),
        in_specs=[
            pl.BlockSpec(memory_space=pltpu.SMEM),
            pl.BlockSpec((4, ATTN_HEAD_DIM), lambda g, i: (0, 0)),
            pl.BlockSpec((tq, w), lambda g, i: (i, g)),
            pl.BlockSpec((None, s, w), lambda g, i: (g, 0, 0)),
            pl.BlockSpec((hps * ATTN_VDIM, s), lambda g, i: (g, 0)),
            pl.BlockSpec((ATTN_VDIM, 1), lambda g, i: (0, 0)),
        ],
        out_specs=pl.BlockSpec((tq, w), lambda g, i: (i, g)),
        out_shape=jax.ShapeDtypeStruct((s, D_ATTN), BF16),
        scratch_shapes=[pltpu.VMEM((hps, 2, ACC_ROWS, tq), F32),
                        pltpu.VMEM((2, hps, 2, tq, tq), F32),
                        pltpu.VMEM((2, hps, 2, tq, tq), BF16)],
        compiler_params=_params("parallel", "arbitrary"),
        name="diff_attention",
    )(slopes, dl, proj, k_groups, vt, subw_col)


def _dot3(a, b_bf16):
    out = None
    for t in _split3(a):
        d = jnp.dot(t, b_bf16, preferred_element_type=F32)
        out = d if out is None else out + d
    return out


def _silu(x):
    half = 0.5 * x
    return half + half * jnp.tanh(half)


def _ssd_kernel(z_ref, xr_ref, bcr_ref, dtr_ref, cw_ref, cb_ref, dtb_ref, alog_ref,
                dskip_ref, nw_ref, o_ref, ext_ref, shift_ref, state_ref):
    L = SSD_CHUNK
    c = pl.program_id(0)

    @pl.when(c == 0)
    def _():
        ext_ref[0:L, :] = jnp.zeros((L, D_XBC), BF16)
        r = lax.broadcasted_iota(jnp.int32, shift_ref.shape, 0)
        cc = lax.broadcasted_iota(jnp.int32, shift_ref.shape, 1)
        shift_ref[...] = jnp.where(cc == L - (CONV_WIDTH - 1) + (r % L) + r // L,
                                   1.0, 0.0).astype(BF16)
        state_ref[...] = jnp.zeros_like(state_ref)

    ext_ref[L:2 * L, 0:D_SSM] = xr_ref[...]
    ext_ref[L:2 * L, D_SSM:D_XBC] = bcr_ref[...]
    shifted = jnp.dot(shift_ref[...], ext_ref[...], preferred_element_type=F32)
    conv = cb_ref[...] + cw_ref[CONV_WIDTH - 1:CONV_WIDTH, :] * ext_ref[L:2 * L, :].astype(F32)
    for t in range(CONV_WIDTH - 1):
        conv = conv + cw_ref[t:t + 1, :] * shifted[t * L:(t + 1) * L]
    ext_ref[L - 16:L, :] = ext_ref[2 * L - 16:2 * L, :]
    xbc = _silu(conv)
    xs = xbc[:, :D_SSM]

    dt = dtr_ref[...] + dtb_ref[...]
    dt = jnp.maximum(dt, 0.0) + jnp.log1p(jnp.exp(-jnp.abs(dt)))
    a = -jnp.exp(alog_ref[...]) * dt
    row = lax.broadcasted_iota(jnp.int32, (L, L), 0)
    col = lax.broadcasted_iota(jnp.int32, (L, L), 1)
    causal = col <= row
    tril = jnp.where(causal, 1.0, 0.0).astype(BF16)
    a_cs = None
    for t in _split3(a):
        d = jnp.dot(tril, t, preferred_element_type=F32)
        a_cs = d if a_cs is None else a_cs + d
    a_cs_t = a_cs.T
    a_end = a_cs[L - 1:L, :]

    hrow = lax.broadcasted_iota(jnp.int32, (LANES, D_SSM), 0)
    ccol = lax.broadcasted_iota(jnp.int32, (LANES, D_SSM), 1)
    expand = jnp.where(ccol // SSM_HEAD_DIM == hrow, 1.0, 0.0).astype(BF16)
    dt_x = _dot3(dt, expand)
    grow_x = _dot3(jnp.exp(a_cs), expand)
    tail_x = _dot3(jnp.exp(a_end - a_cs), expand)
    end_x = grow_x[L - 1:L, :]

    xd = xs * dt_x
    xd_b = xd.astype(BF16)
    xdt_b = (xd * tail_x).astype(BF16)
    lane = lax.broadcasted_iota(jnp.int32, (L, LANES), 1)
    lo_half = lane < SSM_HEAD_DIM
    zeros_b = jnp.zeros((L, LANES), BF16)

    y_parts = []
    for g in range(SSM_GROUPS):
        b_g = xbc[:, D_SSM + g * SSM_STATE:D_SSM + (g + 1) * SSM_STATE].astype(BF16)
        c_g = xbc[:, D_SSM + (SSM_GROUPS + g) * SSM_STATE:
                  D_SSM + (SSM_GROUPS + g + 1) * SSM_STATE].astype(BF16)
        cb = lax.dot_general(c_g, b_g, (((1,), (1,)), ((), ())),
                             preferred_element_type=F32)
        for pr in range(2):
            pair = 2 * g + pr
            lanes = slice(pair * LANES, (pair + 1) * LANES)
            xd_p = xd_b[:, lanes]
            y_p = None
            for hh in range(2):
                head = 2 * pair + hh
                seg = a_cs[:, head:head + 1] - a_cs_t[head:head + 1, :]
                decay = jnp.exp(jnp.where(causal, seg, NEG_BIG))
                w = (cb * decay).astype(BF16)
                xh = jnp.where(lo_half if hh == 0 else ~lo_half, xd_p, zeros_b)
                d = jnp.dot(w, xh, preferred_element_type=F32)
                y_p = d if y_p is None else y_p + d
            st = state_ref[pair]
            y_off = jnp.dot(c_g, st.astype(BF16), preferred_element_type=F32)
            y_parts.append(y_p + y_off * grow_x[:, lanes])
            new = lax.dot_general(b_g, xdt_b[:, lanes], (((0,), (0,)), ((), ())),
                                  preferred_element_type=F32)
            state_ref[pair] = st * end_x[:, lanes] + new

    y = jnp.concatenate(y_parts, axis=-1) + xs * dskip_ref[...]
    gated = y * _silu(z_ref[...].astype(F32))
    ms = jnp.mean(gated * gated, axis=-1, keepdims=True)
    o_ref[...] = (gated * lax.rsqrt(ms + EPS) * nw_ref[...]).astype(o_ref.dtype)


def _ssd_mixer(proj, dt_raw, conv_w, conv_b, dt_bias, a_log, d_skip_x, norm_w):
    s = proj.shape[0]
    L = SSD_CHUNK
    full = lambda shape: pl.BlockSpec(shape, lambda c: (0, 0))
    return pl.pallas_call(
        _ssd_kernel,
        grid=(s // L,),
        in_specs=[
            pl.BlockSpec((L, D_SSM), lambda c: (c, 1)),
            pl.BlockSpec((L, D_SSM), lambda c: (c, 2)),
            pl.BlockSpec((L, D_SSM), lambda c: (c, 3)),
            pl.BlockSpec((L, LANES), lambda c: (c, 0)),
            full((CONV_WIDTH, D_XBC)),
            full((1, D_XBC)),
            full((1, LANES)),
            full((1, LANES)),
            full((1, D_SSM)),
            full((1, D_SSM)),
        ],
        out_specs=pl.BlockSpec((L, D_SSM), lambda c: (c, 0)),
        out_shape=jax.ShapeDtypeStruct((s, D_SSM), BF16),
        scratch_shapes=[pltpu.VMEM((2 * L, D_XBC), BF16),
                        pltpu.VMEM(((CONV_WIDTH - 1) * L, 2 * L), BF16),
                        pltpu.VMEM((N_SSM_HEADS // 2, SSM_STATE, LANES), F32)],
        compiler_params=_params("arbitrary"),
        name="ssd_mixer",
    )(proj, proj, proj, dt_raw, conv_w, conv_b, dt_bias, a_log, d_skip_x, norm_w)


def _out_proj_kernel(a_ref, b_ref, wa_ref, wb_ref, x_ref, o_ref):
    acc = jnp.dot(a_ref[...], wa_ref[...], preferred_element_type=F32)
    acc = acc + jnp.dot(b_ref[...], wb_ref[...], preferred_element_type=F32)
    o_ref[...] = x_ref[...] + acc


def _out_proj(a, b, w, x, tm, tn):
    s = x.shape[0]
    ka, kb = a.shape[1], b.shape[1]
    return pl.pallas_call(
        _out_proj_kernel,
        grid=(s // tm, D_MODEL // tn),
        in_specs=[
            pl.BlockSpec((tm, ka), lambda i, j: (i, 0)),
            pl.BlockSpec((tm, kb), lambda i, j: (i, 0)),
            pl.BlockSpec((ka, tn), lambda i, j: (0, j)),
            pl.BlockSpec((kb, tn), lambda i, j: (1, j)),
            pl.BlockSpec((tm, tn), lambda i, j: (i, j)),
        ],
        out_specs=pl.BlockSpec((tm, tn), lambda i, j: (i, j)),
        out_shape=jax.ShapeDtypeStruct((s, D_MODEL), F32),
        compiler_params=_params("parallel", "arbitrary"),
        name="out_proj",
    )(a, b, w, w, x)


def _ffn_kernel(x_ref, nw_ref, wg_ref, wu_ref, wd_ref, fw_ref, o_ref, h_ref, *, final_norm):
    f = pl.program_id(1)

    @pl.when(f == 0)
    def _():
        xf = x_ref[...]
        ms = jnp.mean(xf * xf, axis=-1, keepdims=True)
        h_ref[...] = (xf * lax.rsqrt(ms + EPS) * nw_ref[...]).astype(BF16)
        o_ref[...] = xf

    h = h_ref[...]
    g = jnp.dot(h, wg_ref[...], preferred_element_type=F32)
    u = jnp.dot(h, wu_ref[...], preferred_element_type=F32)
    act = (_silu(g) * u).astype(BF16)
    o_ref[...] += jnp.dot(act, wd_ref[...], preferred_element_type=F32)

    if final_norm:
        @pl.when(f == pl.num_programs(1) - 1)
        def _():
            xf = o_ref[...]
            ms = jnp.mean(xf * xf, axis=-1, keepdims=True)
            o_ref[...] = xf * lax.rsqrt(ms + EPS) * fw_ref[...]


def _ffn(x, nw, w_gu, w_d, final_w, final_norm, tm, tf):
    s, d = x.shape
    nf = D_FF // tf
    kern = functools.partial(_ffn_kernel, final_norm=final_norm)
    return pl.pallas_call(
        kern,
        grid=(s // tm, nf),
        in_specs=[
            pl.BlockSpec((tm, d), lambda i, f: (i, 0)),
            pl.BlockSpec((1, d), lambda i, f: (0, 0)),
            pl.BlockSpec((d, tf), lambda i, f: (0, f)),
            pl.BlockSpec((d, tf), lambda i, f: (0, nf + f)),
            pl.BlockSpec((tf, d), lambda i, f: (f, 0)),
            pl.BlockSpec((1, d), lambda i, f: (0, 0)),
        ],
        out_specs=pl.BlockSpec((tm, d), lambda i, f: (i, 0)),
        out_shape=jax.ShapeDtypeStruct((s, d), F32),
        scratch_shapes=[pltpu.VMEM((tm, d), BF16)],
        compiler_params=_params("parallel", "arbitrary"),
        name="ffn",
    )(x, nw, w_gu, w_gu, w_d, final_w)


def _alibi_slopes(n_heads):
    start = 2.0 ** (-8.0 / n_heads)
    return start ** jnp.arange(1, n_heads + 1, dtype=F32)


def kernel(x, norm_mix_w, w_in, diff_lambda, subln_w, conv_w, conv_b, dt_bias, a_log,
           d_skip, ssm_norm_w, w_out, norm_ffn_w, w_gate_up, w_down, norm_final_w):
    b, s_len, _ = x.shape
    assert b == 1
    depth = w_in.shape[0]
    o_k, o_v, o_z = D_ATTN, 2 * D_ATTN, 3 * D_ATTN
    o_dt = o_z + D_SSM + D_XBC
    n_dt = w_in.shape[2] - o_dt
    tm = min(512, s_len)
    tm_big = min(1024, s_len)
    tq = min(512, s_len)
    hps = ATTN_HEADS_PER_STEP

    slopes = _alibi_slopes(N_ATTN_HEADS)
    q_scale = ATTN_HEAD_DIM ** -0.5 * LOG2E
    colscale = jnp.concatenate([jnp.full((1, D_ATTN), q_scale, F32),
                                jnp.ones((1, o_dt - o_z), F32)], axis=1)
    pad_l = lambda v: jnp.pad(v.astype(F32), (0, LANES - v.shape[0]))[None, :]
    w_in_b = w_in.astype(BF16)

    xc = x[0]
    for l in range(depth):
        w_main = jnp.concatenate([w_in_b[l, :, :o_k], w_in_b[l, :, o_z:o_dt]], axis=1)
        w_k = w_in_b[l, :, o_k:o_v]
        w_vt = w_in_b[l, :, o_v:o_z].T
        w_dt = jnp.pad(w_in_b[l, :, o_dt:], ((0, 0), (0, LANES - n_dt)))
        nw = norm_mix_w[l][None, :]
        proj = _rms_matmul(xc, nw, w_main, colscale, BF16, tm_big, 1024)
        k_groups, vt, dt_raw = _kv_dt_proj(xc, nw, w_k, w_vt, w_dt, hps * LANES, tm)

        lam_init = 0.8 - 0.6 * math.exp(-0.3 * l)
        attn_out = _diff_attention(proj, k_groups, vt, slopes, diff_lambda[l], subln_w[l][:, None],
                                   lam_init, tq, hps)
        ssm_out = _ssd_mixer(proj, dt_raw, conv_w[l], conv_b[l][None, :], pad_l(dt_bias[l]),
                             pad_l(a_log[l]), jnp.repeat(d_skip[l], SSM_HEAD_DIM)[None, :],
                             ssm_norm_w[l][None, :])
        xc = _out_proj(attn_out, ssm_out, w_out[l].astype(BF16), xc, tm, D_MODEL)
        xc = _ffn(xc, norm_ffn_w[l][None, :], w_gate_up[l].astype(BF16),
                  w_down[l].astype(BF16), norm_final_w[None, :], l == depth - 1, tm_big, 512)
    return xc[None]
```

```python
import functools
import math

import jax
import jax.numpy as jnp
from jax import lax
from jax.experimental import pallas as pl
from jax.experimental.pallas import tpu as pltpu

D_MODEL = 2048
CHUNK = 64
D_ATTN = D_MODEL // 2
D_SSM = D_MODEL - D_ATTN
ATTN_HEAD_DIM = 64
ATTN_VDIM = 2 * ATTN_HEAD_DIM
N_ATTN_HEADS = D_ATTN // ATTN_VDIM
SSM_HEAD_DIM = 64
N_SSM_HEADS = D_SSM // SSM_HEAD_DIM
SSM_STATE = 128
SSM_GROUPS = 4
CONV_WIDTH = 4
SSD_CHUNK = 128
D_XBC = D_SSM + 2 * SSM_GROUPS * SSM_STATE
D_FF = ((8 * D_MODEL // 3 + 255) // 256) * 256
EPS = 1e-5

LANES = 128
LOG2E = math.log2(math.e)
NEG_BIG = -1e30
VMEM_LIMIT = 56 * 1024 * 1024
ATTN_HEADS_PER_STEP = 2

F32 = jnp.float32
BF16 = jnp.bfloat16


def _params(*sem):
    return pltpu.CompilerParams(dimension_semantics=sem, vmem_limit_bytes=VMEM_LIMIT)


def _rms_matmul_kernel(x_ref, nw_ref, w_ref, cs_ref, o_ref, h_ref):
    @pl.when(pl.program_id(1) == 0)
    def _():
        xf = x_ref[...]
        ms = jnp.mean(xf * xf, axis=-1, keepdims=True)
        h_ref[...] = (xf * lax.rsqrt(ms + EPS) * nw_ref[...]).astype(BF16)

    acc = jnp.dot(h_ref[...], w_ref[...], preferred_element_type=F32)
    o_ref[...] = (acc * cs_ref[...]).astype(o_ref.dtype)


def _rms_matmul(x, nw, w_stack, layer, col_blocks, colscale, out_dtype, tm, tn):
    s, d = x.shape
    n = len(col_blocks) * tn
    first, gap = col_blocks[0], col_blocks[1] - col_blocks[0] - 1
    assert tuple(col_blocks) == (first,) + tuple(first + gap + j for j in range(1, len(col_blocks)))
    return pl.pallas_call(
        _rms_matmul_kernel,
        grid=(s // tm, n // tn),
        in_specs=[
            pl.BlockSpec((tm, d), lambda i, j: (i, 0)),
            pl.BlockSpec((1, d), lambda i, j: (0, 0)),
            pl.BlockSpec((None, d, tn),
                         lambda i, j: (layer, 0, first + j + gap * jnp.minimum(j, 1))),
            pl.BlockSpec((1, tn), lambda i, j: (0, j)),
        ],
        out_specs=pl.BlockSpec((tm, tn), lambda i, j: (i, j)),
        out_shape=jax.ShapeDtypeStruct((s, n), out_dtype),
        scratch_shapes=[pltpu.VMEM((tm, d), BF16)],
        compiler_params=_params("parallel", "arbitrary"),
        name="rms_matmul",
    )(x, nw, w_stack, colscale)


def _kv_dt_kernel(x_ref, nw_ref, wk_ref, wvt_ref, wdt_ref, k_ref, vt_ref, dt_ref):
    xf = x_ref[...]
    ms = jnp.mean(xf * xf, axis=-1, keepdims=True)
    h = (xf * lax.rsqrt(ms + EPS) * nw_ref[...]).astype(BF16)
    k = jnp.dot(h, wk_ref[...], preferred_element_type=F32)
    gw = k_ref.shape[2]
    for grp in range(k_ref.shape[0]):
        k_ref[grp] = k[:, grp * gw:(grp + 1) * gw].astype(k_ref.dtype)
    vt_ref[...] = lax.dot_general(wvt_ref[...], h, (((1,), (1,)), ((), ())),
                                  preferred_element_type=F32).astype(vt_ref.dtype)
    dt_ref[...] = jnp.dot(h, wdt_ref[...], preferred_element_type=F32)


def _kv_dt_proj(x, nw, w_stack, layer, k_block, nk, wvt, wdt, gw, tm):
    s, d = x.shape
    nv, ndt = wvt.shape[0], wdt.shape[1]
    full = lambda shape: pl.BlockSpec(shape, lambda i: (0, 0))
    return pl.pallas_call(
        _kv_dt_kernel,
        grid=(s // tm,),
        in_specs=[
            pl.BlockSpec((tm, d), lambda i: (i, 0)),
            full((1, d)),
            pl.BlockSpec((None, d, nk), lambda i: (layer, 0, k_block)),
            full((nv, d)),
            full((d, ndt)),
        ],
        out_specs=[
            pl.BlockSpec((nk // gw, tm, gw), lambda i: (0, i, 0)),
            pl.BlockSpec((nv, tm), lambda i: (0, i)),
            pl.BlockSpec((tm, ndt), lambda i: (i, 0)),
        ],
        out_shape=[
            jax.ShapeDtypeStruct((nk // gw, s, gw), BF16),
            jax.ShapeDtypeStruct((nv, s), BF16),
            jax.ShapeDtypeStruct((s, ndt), F32),
        ],
        compiler_params=_params("parallel"),
        name="kv_dt_proj",
    )(x, nw, w_stack, wvt, wdt)


ONES_ROWS = 16
ACC_ROWS = ATTN_VDIM + ONES_ROWS


def _split3(a):
    hi = a.astype(BF16)
    r1 = a - hi.astype(F32)
    mid = r1.astype(BF16)
    lo = (r1 - mid.astype(F32)).astype(BF16)
    return hi, mid, lo


def _attn_kernel(slopes_ref, dl_ref, q_ref, k_ref, vt_ref, sw_ref, o_ref,
                 acc_ref, s_ref, p_ref, *, tq, hps, lam_init):
    g = pl.program_id(0)
    i = pl.program_id(1)
    heads = range(hps)
    hm = [(hh, j) for hh in heads for j in range(2)]
    cs = [slopes_ref[g * hps + hh] * LOG2E for hh in heads]

    shape = (tq, LANES)
    lane = lax.broadcasted_iota(jnp.int32, shape, 1)
    zero = jnp.zeros(shape, BF16)
    krow = lax.broadcasted_iota(jnp.int32, shape, 0)
    k_split = jnp.where((lane & 1) == 0, (krow >> 4) << 4, krow & 15)
    k_bias = jnp.where(lane < 6, k_split, 0).astype(F32).astype(BF16)
    q_maps = {}
    for hh in heads:
        q = q_ref[:, hh * LANES:(hh + 1) * LANES]
        c_terms = _split3(jnp.full(shape, cs[hh], F32))
        q_bias = jnp.zeros(shape, F32)
        for t in range(3):
            q_bias = jnp.where((lane >> 1) == t, c_terms[t].astype(F32), q_bias)
        q_bias = q_bias.astype(BF16)
        q_maps[hh, 0] = jnp.concatenate([jnp.where(lane < ATTN_HEAD_DIM, q, zero), q_bias], axis=1)
        q_maps[hh, 1] = jnp.concatenate([jnp.where(lane >= ATTN_HEAD_DIM, q, zero), q_bias], axis=1)

    acc_ref[...] = jnp.zeros_like(acc_ref)
    ones_rows = jnp.ones((ONES_ROWS, tq), BF16)

    def scores(blk, slot):
        k0 = pl.multiple_of(blk * tq, tq)
        col_max = {}
        for hh in heads:
            kblk = jnp.concatenate([k_ref[pl.ds(k0, tq), hh * LANES:(hh + 1) * LANES], k_bias],
                                   axis=1)
            for j in range(2):
                s = lax.dot_general(kblk, q_maps[hh, j], (((1,), (1,)), ((), ())),
                                    preferred_element_type=F32)
                s_ref[slot, hh, j] = s
                col_max[hh, j] = jnp.max(s, axis=0, keepdims=True)
        return col_max

    def softmax(slot, bias, offs, ms, col_max):
        new_ms, alphas = {}, {}
        for hh, j in hm:
            s = s_ref[slot, hh, j]
            if bias is None:
                s_max = col_max[hh, j]
            else:
                s = s + bias[hh]
                s_max = jnp.max(s, axis=0, keepdims=True)
            m_new = jnp.maximum(ms[hh, j], s_max + offs[hh])
            alphas[hh, j] = jnp.exp2(ms[hh, j] - m_new)
            p_ref[slot, hh, j] = jnp.exp2(s - (m_new - offs[hh])).astype(BF16)
            new_ms[hh, j] = m_new
        return new_ms, alphas

    def values(blk, slot, alphas):
        k0 = pl.multiple_of(blk * tq, tq)
        for hh in heads:
            lhs = jnp.concatenate([vt_ref[hh * ATTN_VDIM:(hh + 1) * ATTN_VDIM, pl.ds(k0, tq)],
                                   ones_rows], axis=0)
            for j in range(2):
                pv = jnp.dot(lhs, p_ref[slot, hh, j], preferred_element_type=F32)
                acc_ref[hh, j] = alphas[hh, j] * acc_ref[hh, j] + pv

    init_m = {k: jnp.full((1, tq), NEG_BIG, F32) for k in hm}
    init_a = {k: jnp.ones((1, tq), F32) for k in hm}
    col_max0 = scores(0, 0)

    def past_block(t, slot, has_prev, carry):
        ms, prev_alphas, col_max = carry
        dist = ((i - t) * tq).astype(F32)
        ms, alphas = softmax(slot, None, [-cs[hh] * dist for hh in heads], ms, col_max)
        col_max = scores(t + 1, 1 - slot)
        if has_prev:
            values(t - 1, 1 - slot, prev_alphas)
        return ms, alphas, col_max

    def later_step(t, carry):
        return lax.cond((t & 1) == 0, functools.partial(past_block, t, 0, True),
                        functools.partial(past_block, t, 1, True), carry)

    def past_step(t, carry):
        return lax.cond(t == 0, functools.partial(past_block, t, 0, False),
                        functools.partial(later_step, t), carry)

    carry = lax.fori_loop(0, i, past_step, (init_m, init_a, col_max0))

    dl = dl_ref[...]
    lam = (jnp.exp(jnp.sum(dl[0:1] * dl[1:2], axis=-1, keepdims=True))
           - jnp.exp(jnp.sum(dl[2:3] * dl[3:4], axis=-1, keepdims=True)) + lam_init)

    def finish(slot, has_prev):
        ms, alphas, _ = carry
        if has_prev:
            values(i - 1, 1 - slot, alphas)
        kl = lax.broadcasted_iota(jnp.int32, (tq, tq), 0)
        ql = lax.broadcasted_iota(jnp.int32, (tq, tq), 1)
        allowed = (kl // CHUNK) <= (ql // CHUNK)
        ahead = jnp.maximum(kl - ql, 0).astype(F32)
        bias = [jnp.where(allowed, (-2.0 * cs[hh]) * ahead, NEG_BIG) for hh in heads]
        ms, alphas = softmax(slot, bias, [0.0] * hps, ms, None)
        values(i, slot, alphas)
        for hh in heads:
            a1, a2 = acc_ref[hh, 0], acc_ref[hh, 1]
            o = (a1[:ATTN_VDIM] * (1.0 / a1[ATTN_VDIM:ATTN_VDIM + 1])
                 - lam * (a2[:ATTN_VDIM] * (1.0 / a2[ATTN_VDIM:ATTN_VDIM + 1])))
            ms_o = jnp.mean(o * o, axis=0, keepdims=True)
            y = (o * lax.rsqrt(ms_o + EPS) * sw_ref[...]) * (1.0 - lam_init)
            o_ref[:, hh * LANES:(hh + 1) * LANES] = y.T.astype(o_ref.dtype)

    pl.when(i == 0)(functools.partial(finish, 0, False))
    pl.when((i > 0) & ((i & 1) == 0))(functools.partial(finish, 0, True))
    pl.when((i & 1) == 1)(functools.partial(finish, 1, True))


def _diff_attention(proj, k_groups, vt, slopes, dl, subw_col, lam_init, tq, hps):
    s = proj.shape[0]
    w = hps * LANES
    kern = functools.partial(_attn_kernel, tq=tq, hps=hps, lam_init=lam_init)
    return pl.pallas_call(
        kern,
        grid=(N_ATTN_HEADS // hps, s // tq),
        in_specs=[
            pl.BlockSpec(memory_space=pltpu.SMEM),
            pl.BlockSpec((4, ATTN_HEAD_DIM), lambda g, i: (0, 0)),
            pl.BlockSpec((tq, w), lambda g, i: (i, g)),
            pl.BlockSpec((None, s, w), lambda g, i: (g, 0, 0)),
            pl.BlockSpec((hps * ATTN_VDIM, s), lambda g, i: (g, 0)),
            pl.BlockSpec((ATTN_VDIM, 1), lambda g, i: (0, 0)),
        ],
        out_specs=pl.BlockSpec((tq, w), lambda g, i: (i, g)),
        out_shape=jax.ShapeDtypeStruct((s, D_ATTN), BF16),
        scratch_shapes=[pltpu.VMEM((hps, 2, ACC_ROWS, tq), F32),
                        pltpu.VMEM((2, hps, 2, tq, tq), F32),
                        pltpu.VMEM((2, hps, 2, tq, tq), BF16)],
        compiler_params=_params("parallel", "arbitrary"),
        name="diff_attention",
    )(slopes, dl, proj, k_groups, vt, subw_col)


def _dot3(a, b_bf16):
    out = None
    for t in _split3(a):
        d = jnp.dot(t, b_bf16, preferred_element_type=F32)
        out = d if out is None else out + d
    return out


def _silu(x):
    half = 0.5 * x
    return half + half * jnp.tanh(half)


def _ssd_kernel(z_ref, xr_ref, bcr_ref, dtr_ref, cw_ref, cb_ref, dtb_ref, alog_ref,
                dskip_ref, nw_ref, o_ref, ext_ref, shift_ref, state_ref):
    L = SSD_CHUNK
    c = pl.program_id(0)

    @pl.when(c == 0)
    def _():
        ext_ref[0:L, :] = jnp.zeros((L, D_XBC), BF16)
        r = lax.broadcasted_iota(jnp.int32, shift_ref.shape, 0)
        cc = lax.broadcasted_iota(jnp.int32, shift_ref.shape, 1)
        shift_ref[...] = jnp.where(cc == L - (CONV_WIDTH - 1) + (r % L) + r // L,
                                   1.0, 0.0).astype(BF16)
        state_ref[...] = jnp.zeros_like(state_ref)

    ext_ref[L:2 * L, 0:D_SSM] = xr_ref[...]
    ext_ref[L:2 * L, D_SSM:D_XBC] = bcr_ref[...]
    shifted = jnp.dot(shift_ref[...], ext_ref[...], preferred_element_type=F32)
    conv = cb_ref[...] + cw_ref[CONV_WIDTH - 1:CONV_WIDTH, :] * ext_ref[L:2 * L, :].astype(F32)
    for t in range(CONV_WIDTH - 1):
        conv = conv + cw_ref[t:t + 1, :] * shifted[t * L:(t + 1) * L]
    ext_ref[L - 16:L, :] = ext_ref[2 * L - 16:2 * L, :]
    xbc = _silu(conv)
    xs = xbc[:, :D_SSM]

    dt = dtr_ref[...] + dtb_ref[...]
    dt = jnp.maximum(dt, 0.0) + jnp.log1p(jnp.exp(-jnp.abs(dt)))
    a = -jnp.exp(alog_ref[...]) * dt
    row = lax.broadcasted_iota(jnp.int32, (L, L), 0)
    col = lax.broadcasted_iota(jnp.int32, (L, L), 1)
    causal = col <= row
    tril = jnp.where(causal, 1.0, 0.0).astype(BF16)
    a_cs = None
    for t in _split3(a):
        d = jnp.dot(tril, t, preferred_element_type=F32)
        a_cs = d if a_cs is None else a_cs + d
    a_cs_t = a_cs.T
    a_end = a_cs[L - 1:L, :]

    hrow = lax.broadcasted_iota(jnp.int32, (LANES, D_SSM), 0)
    ccol = lax.broadcasted_iota(jnp.int32, (LANES, D_SSM), 1)
    expand = jnp.where(ccol // SSM_HEAD_DIM == hrow, 1.0, 0.0).astype(BF16)
    dt_x = _dot3(dt, expand)
    grow_x = _dot3(jnp.exp(a_cs), expand)
    tail_x = _dot3(jnp.exp(a_end - a_cs), expand)
    end_x = grow_x[L - 1:L, :]

    xd = xs * dt_x
    xd_b = xd.astype(BF16)
    xdt_b = (xd * tail_x).astype(BF16)
    lane = lax.broadcasted_iota(jnp.int32, (L, LANES), 1)
    lo_half = lane < SSM_HEAD_DIM
    zeros_b = jnp.zeros((L, LANES), BF16)

    y_parts = []
    for g in range(SSM_GROUPS):
        b_g = xbc[:, D_SSM + g * SSM_STATE:D_SSM + (g + 1) * SSM_STATE].astype(BF16)
        c_g = xbc[:, D_SSM + (SSM_GROUPS + g) * SSM_STATE:
                  D_SSM + (SSM_GROUPS + g + 1) * SSM_STATE].astype(BF16)
        cb = lax.dot_general(c_g, b_g, (((1,), (1,)), ((), ())),
                             preferred_element_type=F32)
        for pr in range(2):
            pair = 2 * g + pr
            lanes = slice(pair * LANES, (pair + 1) * LANES)
            xd_p = xd_b[:, lanes]
            y_p = None
            for hh in range(2):
                head = 2 * pair + hh
                seg = a_cs[:, head:head + 1] - a_cs_t[head:head + 1, :]
                decay = jnp.exp(jnp.where(causal, seg, NEG_BIG))
                w = (cb * decay).astype(BF16)
                xh = jnp.where(lo_half if hh == 0 else ~lo_half, xd_p, zeros_b)
                d = jnp.dot(w, xh, preferred_element_type=F32)
                y_p = d if y_p is None else y_p + d
            st = state_ref[pair]
            y_off = jnp.dot(c_g, st.astype(BF16), preferred_element_type=F32)
            y_parts.append(y_p + y_off * grow_x[:, lanes])
            new = lax.dot_general(b_g, xdt_b[:, lanes], (((0,), (0,)), ((), ())),
                                  preferred_element_type=F32)
            state_ref[pair] = st * end_x[:, lanes] + new

    y = jnp.concatenate(y_parts, axis=-1) + xs * dskip_ref[...]
    gated = y * _silu(z_ref[...].astype(F32))
    ms = jnp.mean(gated * gated, axis=-1, keepdims=True)
    o_ref[...] = (gated * lax.rsqrt(ms + EPS) * nw_ref[...]).astype(o_ref.dtype)


def _ssd_mixer(proj, dt_raw, conv_w, conv_b, dt_bias, a_log, d_skip_x, norm_w):
    s = proj.shape[0]
    L = SSD_CHUNK
    full = lambda shape: pl.BlockSpec(shape, lambda c: (0, 0))
    return pl.pallas_call(
        _ssd_kernel,
        grid=(s // L,),
        in_specs=[
            pl.BlockSpec((L, D_SSM), lambda c: (c, 1)),
            pl.BlockSpec((L, D_SSM), lambda c: (c, 2)),
            pl.BlockSpec((L, D_SSM), lambda c: (c, 3)),
            pl.BlockSpec((L, LANES), lambda c: (c, 0)),
            full((CONV_WIDTH, D_XBC)),
            full((1, D_XBC)),
            full((1, LANES)),
            full((1, LANES)),
            full((1, D_SSM)),
            full((1, D_SSM)),
        ],
        out_specs=pl.BlockSpec((L, D_SSM), lambda c: (c, 0)),
        out_shape=jax.ShapeDtypeStruct((s, D_SSM), BF16),
        scratch_shapes=[pltpu.VMEM((2 * L, D_XBC), BF16),
                        pltpu.VMEM(((CONV_WIDTH - 1) * L, 2 * L), BF16),
                        pltpu.VMEM((N_SSM_HEADS // 2, SSM_STATE, LANES), F32)],
        compiler_params=_params("arbitrary"),
        name="ssd_mixer",
    )(proj, proj, proj, dt_raw, conv_w, conv_b, dt_bias, a_log, d_skip_x, norm_w)


def _out_proj_kernel(a_ref, b_ref, wa_ref, wb_ref, x_ref, o_ref):
    acc = jnp.dot(a_ref[...], wa_ref[...], preferred_element_type=F32)
    acc = acc + jnp.dot(b_ref[...], wb_ref[...], preferred_element_type=F32)
    o_ref[...] = x_ref[...] + acc


def _out_proj(a, b, w_stack, layer, x, tm, tn):
    s = x.shape[0]
    ka, kb = a.shape[1], b.shape[1]
    assert ka == kb
    return pl.pallas_call(
        _out_proj_kernel,
        grid=(s // tm, D_MODEL // tn),
        in_specs=[
            pl.BlockSpec((tm, ka), lambda i, j: (i, 0)),
            pl.BlockSpec((tm, kb), lambda i, j: (i, 0)),
            pl.BlockSpec((None, ka, tn), lambda i, j: (layer, 0, j)),
            pl.BlockSpec((None, kb, tn), lambda i, j: (layer, 1, j)),
            pl.BlockSpec((tm, tn), lambda i, j: (i, j)),
        ],
        out_specs=pl.BlockSpec((tm, tn), lambda i, j: (i, j)),
        out_shape=jax.ShapeDtypeStruct((s, D_MODEL), F32),
        compiler_params=_params("parallel", "arbitrary"),
        name="out_proj",
    )(a, b, w_stack, w_stack, x)


def _ffn_kernel(x_ref, nw_ref, wg_ref, wu_ref, wd_ref, fw_ref, o_ref, h_ref, *, final_norm):
    f = pl.program_id(1)

    @pl.when(f == 0)
    def _():
        xf = x_ref[...]
        ms = jnp.mean(xf * xf, axis=-1, keepdims=True)
        h_ref[...] = (xf * lax.rsqrt(ms + EPS) * nw_ref[...]).astype(BF16)
        o_ref[...] = xf

    h = h_ref[...]
    g = jnp.dot(h, wg_ref[...], preferred_element_type=F32)
    u = jnp.dot(h, wu_ref[...], preferred_element_type=F32)
    act = (_silu(g) * u).astype(BF16)
    o_ref[...] += jnp.dot(act, wd_ref[...], preferred_element_type=F32)

    if final_norm:
        @pl.when(f == pl.num_programs(1) - 1)
        def _():
            xf = o_ref[...]
            ms = jnp.mean(xf * xf, axis=-1, keepdims=True)
            o_ref[...] = xf * lax.rsqrt(ms + EPS) * fw_ref[...]


def _ffn(x, nw, w_gu_stack, w_d_stack, layer, final_w, final_norm, tm, tf):
    s, d = x.shape
    nf = D_FF // tf
    kern = functools.partial(_ffn_kernel, final_norm=final_norm)
    return pl.pallas_call(
        kern,
        grid=(s // tm, nf),
        in_specs=[
            pl.BlockSpec((tm, d), lambda i, f: (i, 0)),
            pl.BlockSpec((1, d), lambda i, f: (0, 0)),
            pl.BlockSpec((None, d, tf), lambda i, f: (layer, 0, f)),
            pl.BlockSpec((None, d, tf), lambda i, f: (layer, 0, nf + f)),
            pl.BlockSpec((None, tf, d), lambda i, f: (layer, f, 0)),
            pl.BlockSpec((1, d), lambda i, f: (0, 0)),
        ],
        out_specs=pl.BlockSpec((tm, d), lambda i, f: (i, 0)),
        out_shape=jax.ShapeDtypeStruct((s, d), F32),
        scratch_shapes=[pltpu.VMEM((tm, d), BF16)],
        compiler_params=_params("parallel", "arbitrary"),
        name="ffn",
    )(x, nw, w_gu_stack, w_gu_stack, w_d_stack, final_w)


def _alibi_slopes(n_heads):
    start = 2.0 ** (-8.0 / n_heads)
    return start ** jnp.arange(1, n_heads + 1, dtype=F32)


def kernel(x, norm_mix_w, w_in, diff_lambda, subln_w, conv_w, conv_b, dt_bias, a_log,
           d_skip, ssm_norm_w, w_out, norm_ffn_w, w_gate_up, w_down, norm_final_w):
    b, s_len, _ = x.shape
    assert b == 1
    depth = w_in.shape[0]
    o_k, o_v, o_z = D_ATTN, 2 * D_ATTN, 3 * D_ATTN
    o_dt = o_z + D_SSM + D_XBC
    n_dt = w_in.shape[2] - o_dt
    tm = min(512, s_len)
    tm_big = min(1024, s_len)
    tq = min(512, s_len)
    hps = ATTN_HEADS_PER_STEP

    slopes = _alibi_slopes(N_ATTN_HEADS)
    q_scale = ATTN_HEAD_DIM ** -0.5 * LOG2E
    colscale = jnp.concatenate([jnp.full((1, D_ATTN), q_scale, F32),
                                jnp.ones((1, o_dt - o_z), F32)], axis=1)
    pad_l = lambda v: jnp.pad(v.astype(F32), (0, LANES - v.shape[0]))[None, :]
    w_in_b = w_in.astype(BF16)
    w_out_b = w_out.astype(BF16)
    w_gu_b = w_gate_up.astype(BF16)
    w_d_b = w_down.astype(BF16)
    tn = D_ATTN
    main_blocks = (0,) + tuple(range(o_z // tn, o_dt // tn))

    xc = x[0]
    for l in range(depth):
        w_vt = w_in_b[l, :, o_v:o_z].T
        w_dt = jnp.pad(w_in_b[l, :, o_dt:], ((0, 0), (0, LANES - n_dt)))
        nw = norm_mix_w[l][None, :]
        proj = _rms_matmul(xc, nw, w_in_b, l, main_blocks, colscale, BF16, tm_big, tn)
        k_groups, vt, dt_raw = _kv_dt_proj(xc, nw, w_in_b, l, o_k // tn, tn, w_vt, w_dt,
                                           hps * LANES, tm)

        lam_init = 0.8 - 0.6 * math.exp(-0.3 * l)
        attn_out = _diff_attention(proj, k_groups, vt, slopes, diff_lambda[l], subln_w[l][:, None],
                                   lam_init, tq, hps)
        ssm_out = _ssd_mixer(proj, dt_raw, conv_w[l], conv_b[l][None, :], pad_l(dt_bias[l]),
                             pad_l(a_log[l]), jnp.repeat(d_skip[l], SSM_HEAD_DIM)[None, :],
                             ssm_norm_w[l][None, :])
        xc = _out_proj(attn_out, ssm_out, w_out_b, l, xc, tm, D_MODEL)
        xc = _ffn(xc, norm_ffn_w[l][None, :], w_gu_b, w_d_b, l, norm_final_w[None, :],
                  l == depth - 1, tm_big, 512)
    return xc[None]
```

```python
import functools
import math

import jax
import jax.numpy as jnp
from jax import lax
from jax.experimental import pallas as pl
from jax.experimental.pallas import tpu as pltpu

D_MODEL = 2048
CHUNK = 64
D_ATTN = D_MODEL // 2
D_SSM = D_MODEL - D_ATTN
ATTN_HEAD_DIM = 64
ATTN_VDIM = 2 * ATTN_HEAD_DIM
N_ATTN_HEADS = D_ATTN // ATTN_VDIM
SSM_HEAD_DIM = 64
N_SSM_HEADS = D_SSM // SSM_HEAD_DIM
SSM_STATE = 128
SSM_GROUPS = 4
CONV_WIDTH = 4
SSD_CHUNK = 128
D_XBC = D_SSM + 2 * SSM_GROUPS * SSM_STATE
D_FF = ((8 * D_MODEL // 3 + 255) // 256) * 256
EPS = 1e-5

LANES = 128
LOG2E = math.log2(math.e)
NEG_BIG = -1e30
VMEM_LIMIT = 56 * 1024 * 1024
ATTN_HEADS_PER_STEP = 2

F32 = jnp.float32
BF16 = jnp.bfloat16


def _params(*sem):
    return pltpu.CompilerParams(dimension_semantics=sem, vmem_limit_bytes=VMEM_LIMIT)


def _rms_matmul_kernel(x_ref, nw_ref, w_ref, cs_ref, o_ref, h_ref):
    @pl.when(pl.program_id(1) == 0)
    def _():
        xf = x_ref[...]
        ms = jnp.mean(xf * xf, axis=-1, keepdims=True)
        h_ref[...] = (xf * lax.rsqrt(ms + EPS) * nw_ref[...]).astype(BF16)

    acc = jnp.dot(h_ref[...], w_ref[...], preferred_element_type=F32)
    o_ref[...] = (acc * cs_ref[...]).astype(o_ref.dtype)


def _rms_matmul(x, nw, w_stack, layer, col_blocks, colscale, out_dtype, tm, tn):
    s, d = x.shape
    n = len(col_blocks) * tn
    first, gap = col_blocks[0], col_blocks[1] - col_blocks[0] - 1
    assert tuple(col_blocks) == (first,) + tuple(first + gap + j for j in range(1, len(col_blocks)))
    return pl.pallas_call(
        _rms_matmul_kernel,
        grid=(s // tm, n // tn),
        in_specs=[
            pl.BlockSpec((tm, d), lambda i, j: (i, 0)),
            pl.BlockSpec((1, d), lambda i, j: (0, 0)),
            pl.BlockSpec((None, d, tn),
                         lambda i, j: (layer, 0, first + j + gap * jnp.minimum(j, 1))),
            pl.BlockSpec((1, tn), lambda i, j: (0, j)),
        ],
        out_specs=pl.BlockSpec((tm, tn), lambda i, j: (i, j)),
        out_shape=jax.ShapeDtypeStruct((s, n), out_dtype),
        scratch_shapes=[pltpu.VMEM((tm, d), BF16)],
        compiler_params=_params("parallel", "arbitrary"),
        name="rms_matmul",
    )(x, nw, w_stack, colscale)


def _kv_dt_kernel(x_ref, nw_ref, wk_ref, wvt_ref, wdt_ref, k_ref, vt_ref, dt_ref):
    xf = x_ref[...]
    ms = jnp.mean(xf * xf, axis=-1, keepdims=True)
    h = (xf * lax.rsqrt(ms + EPS) * nw_ref[...]).astype(BF16)
    k = jnp.dot(h, wk_ref[...], preferred_element_type=F32)
    gw = k_ref.shape[2]
    for grp in range(k_ref.shape[0]):
        k_ref[grp] = k[:, grp * gw:(grp + 1) * gw].astype(k_ref.dtype)
    vt_ref[...] = lax.dot_general(wvt_ref[...], h, (((1,), (1,)), ((), ())),
                                  preferred_element_type=F32).astype(vt_ref.dtype)
    dt_ref[...] = jnp.dot(h, wdt_ref[...], preferred_element_type=F32)


def _kv_dt_proj(x, nw, w_stack, layer, k_block, nk, wvt, wdt, gw, tm):
    s, d = x.shape
    nv, ndt = wvt.shape[0], wdt.shape[1]
    full = lambda shape: pl.BlockSpec(shape, lambda i: (0, 0))
    return pl.pallas_call(
        _kv_dt_kernel,
        grid=(s // tm,),
        in_specs=[
            pl.BlockSpec((tm, d), lambda i: (i, 0)),
            full((1, d)),
            pl.BlockSpec((None, d, nk), lambda i: (layer, 0, k_block)),
            full((nv, d)),
            full((d, ndt)),
        ],
        out_specs=[
            pl.BlockSpec((nk // gw, tm, gw), lambda i: (0, i, 0)),
            pl.BlockSpec((nv, tm), lambda i: (0, i)),
            pl.BlockSpec((tm, ndt), lambda i: (i, 0)),
        ],
        out_shape=[
            jax.ShapeDtypeStruct((nk // gw, s, gw), BF16),
            jax.ShapeDtypeStruct((nv, s), BF16),
            jax.ShapeDtypeStruct((s, ndt), F32),
        ],
        compiler_params=_params("parallel"),
        name="kv_dt_proj",
    )(x, nw, w_stack, wvt, wdt)


ONES_ROWS = 16
ACC_ROWS = ATTN_VDIM + ONES_ROWS


def _split3(a):
    hi = a.astype(BF16)
    r1 = a - hi.astype(F32)
    mid = r1.astype(BF16)
    lo = (r1 - mid.astype(F32)).astype(BF16)
    return hi, mid, lo


def _attn_kernel(slopes_ref, dl_ref, q_ref, k_ref, vt_ref, sw_ref, o_ref,
                 acc_ref, s_ref, p_ref, *, tq, hps, lam_init):
    g = pl.program_id(0)
    i = pl.program_id(1)
    heads = range(hps)
    hm = [(hh, j) for hh in heads for j in range(2)]
    cs = [slopes_ref[g * hps + hh] * LOG2E for hh in heads]

    shape = (tq, LANES)
    lane = lax.broadcasted_iota(jnp.int32, shape, 1)
    zero = jnp.zeros(shape, BF16)
    krow = lax.broadcasted_iota(jnp.int32, shape, 0)
    k_split = jnp.where((lane & 1) == 0, (krow >> 4) << 4, krow & 15)
    k_bias = jnp.where(lane < 6, k_split, 0).astype(F32).astype(BF16)
    q_maps = {}
    for hh in heads:
        q = q_ref[:, hh * LANES:(hh + 1) * LANES]
        c_terms = _split3(jnp.full(shape, cs[hh], F32))
        q_bias = jnp.zeros(shape, F32)
        for t in range(3):
            q_bias = jnp.where((lane >> 1) == t, c_terms[t].astype(F32), q_bias)
        q_bias = q_bias.astype(BF16)
        q_maps[hh, 0] = jnp.concatenate([jnp.where(lane < ATTN_HEAD_DIM, q, zero), q_bias], axis=1)
        q_maps[hh, 1] = jnp.concatenate([jnp.where(lane >= ATTN_HEAD_DIM, q, zero), q_bias], axis=1)

    acc_ref[...] = jnp.zeros_like(acc_ref)
    ones_rows = jnp.ones((ONES_ROWS, tq), BF16)

    def scores(blk, slot):
        k0 = pl.multiple_of(blk * tq, tq)
        col_max = {}
        for hh in heads:
            kblk = jnp.concatenate([k_ref[pl.ds(k0, tq), hh * LANES:(hh + 1) * LANES], k_bias],
                                   axis=1)
            for j in range(2):
                s = lax.dot_general(kblk, q_maps[hh, j], (((1,), (1,)), ((), ())),
                                    preferred_element_type=F32)
                s_ref[slot, hh, j] = s
                col_max[hh, j] = jnp.max(s, axis=0, keepdims=True)
        return col_max

    def softmax(slot, bias, offs, ms, col_max):
        new_ms, alphas = {}, {}
        half = tq // 2
        for hh, j in hm:
            if bias is None:
                m_new = jnp.maximum(ms[hh, j], col_max[hh, j] + offs[hh])
                shift = m_new - offs[hh]
                for c0 in (0, half):
                    p_ref[slot, hh, j, :, c0:c0 + half] = jnp.exp2(
                        s_ref[slot, hh, j, :, c0:c0 + half] - shift[:, c0:c0 + half]).astype(BF16)
            else:
                s = s_ref[slot, hh, j] + bias[hh]
                m_new = jnp.maximum(ms[hh, j], jnp.max(s, axis=0, keepdims=True) + offs[hh])
                p_ref[slot, hh, j] = jnp.exp2(s - (m_new - offs[hh])).astype(BF16)
            alphas[hh, j] = jnp.exp2(ms[hh, j] - m_new)
            new_ms[hh, j] = m_new
        return new_ms, alphas

    def values(blk, slot, alphas):
        k0 = pl.multiple_of(blk * tq, tq)
        for hh in heads:
            lhs = jnp.concatenate([vt_ref[hh * ATTN_VDIM:(hh + 1) * ATTN_VDIM, pl.ds(k0, tq)],
                                   ones_rows], axis=0)
            for j in range(2):
                pv = jnp.dot(lhs, p_ref[slot, hh, j], preferred_element_type=F32)
                acc_ref[hh, j] = alphas[hh, j] * acc_ref[hh, j] + pv

    init_m = {k: jnp.full((1, tq), NEG_BIG, F32) for k in hm}
    init_a = {k: jnp.ones((1, tq), F32) for k in hm}
    col_max0 = scores(0, 0)

    def past_block(t, slot, has_prev, carry):
        ms, prev_alphas, col_max = carry
        dist = ((i - t) * tq).astype(F32)
        ms, alphas = softmax(slot, None, [-cs[hh] * dist for hh in heads], ms, col_max)
        col_max = scores(t + 1, 1 - slot)
        if has_prev:
            values(t - 1, 1 - slot, prev_alphas)
        return ms, alphas, col_max

    def later_step(t, carry):
        return lax.cond((t & 1) == 0, functools.partial(past_block, t, 0, True),
                        functools.partial(past_block, t, 1, True), carry)

    def past_step(t, carry):
        return lax.cond(t == 0, functools.partial(past_block, t, 0, False),
                        functools.partial(later_step, t), carry)

    carry = lax.fori_loop(0, i, past_step, (init_m, init_a, col_max0))

    dl = dl_ref[...]
    lam = (jnp.exp(jnp.sum(dl[0:1] * dl[1:2], axis=-1, keepdims=True))
           - jnp.exp(jnp.sum(dl[2:3] * dl[3:4], axis=-1, keepdims=True)) + lam_init)

    def finish(slot, has_prev):
        ms, alphas, _ = carry
        if has_prev:
            values(i - 1, 1 - slot, alphas)
        kl = lax.broadcasted_iota(jnp.int32, (tq, tq), 0)
        ql = lax.broadcasted_iota(jnp.int32, (tq, tq), 1)
        allowed = (kl // CHUNK) <= (ql // CHUNK)
        ahead = jnp.maximum(kl - ql, 0).astype(F32)
        bias = [jnp.where(allowed, (-2.0 * cs[hh]) * ahead, NEG_BIG) for hh in heads]
        ms, alphas = softmax(slot, bias, [0.0] * hps, ms, None)
        values(i, slot, alphas)
        for hh in heads:
            a1, a2 = acc_ref[hh, 0], acc_ref[hh, 1]
            o = (a1[:ATTN_VDIM] * (1.0 / a1[ATTN_VDIM:ATTN_VDIM + 1])
                 - lam * (a2[:ATTN_VDIM] * (1.0 / a2[ATTN_VDIM:ATTN_VDIM + 1])))
            ms_o = jnp.mean(o * o, axis=0, keepdims=True)
            y = (o * lax.rsqrt(ms_o + EPS) * sw_ref[...]) * (1.0 - lam_init)
            o_ref[:, hh * LANES:(hh + 1) * LANES] = y.T.astype(o_ref.dtype)

    pl.when(i == 0)(functools.partial(finish, 0, False))
    pl.when((i > 0) & ((i & 1) == 0))(functools.partial(finish, 0, True))
    pl.when((i & 1) == 1)(functools.partial(finish, 1, True))


def _diff_attention(proj, k_groups, vt, slopes, dl, subw_col, lam_init, tq, hps):
    s = proj.shape[0]
    w = hps * LANES
    kern = functools.partial(_attn_kernel, tq=tq, hps=hps, lam_init=lam_init)
    return pl.pallas_call(
        kern,
        grid=(N_ATTN_HEADS // hps, s // tq),
        in_specs=[
            pl.BlockSpec(memory_space=pltpu.SMEM),
            pl.BlockSpec((4, ATTN_HEAD_DIM), lambda g, i: (0, 0)),
            pl.BlockSpec((tq, w), lambda g, i: (i, g)),
            pl.BlockSpec((None, s, w), lambda g, i: (g, 0, 0)),
            pl.BlockSpec((hps * ATTN_VDIM, s), lambda g, i: (g, 0)),
            pl.BlockSpec((ATTN_VDIM, 1), lambda g, i: (0, 0)),
        ],
        out_specs=pl.BlockSpec((tq, w), lambda g, i: (i, g)),
        out_shape=jax.ShapeDtypeStruct((s, D_ATTN), BF16),
        scratch_shapes=[pltpu.VMEM((hps, 2, ACC_ROWS, tq), F32),
                        pltpu.VMEM((2, hps, 2, tq, tq), F32),
                        pltpu.VMEM((2, hps, 2, tq, tq), BF16)],
        compiler_params=_params("parallel", "arbitrary"),
        name="diff_attention",
    )(slopes, dl, proj, k_groups, vt, subw_col)


def _dot3(a, b_bf16):
    out = None
    for t in _split3(a):
        d = jnp.dot(t, b_bf16, preferred_element_type=F32)
        out = d if out is None else out + d
    return out


def _silu(x):
    half = 0.5 * x
    return half + half * jnp.tanh(half)


def _ssd_kernel(z_ref, xr_ref, bcr_ref, dtr_ref, cw_ref, cb_ref, dtb_ref, alog_ref,
                dskip_ref, nw_ref, o_ref, ext_ref, shift_ref, state_ref):
    L = SSD_CHUNK
    c = pl.program_id(0)

    @pl.when(c == 0)
    def _():
        ext_ref[0:L, :] = jnp.zeros((L, D_XBC), BF16)
        r = lax.broadcasted_iota(jnp.int32, shift_ref.shape, 0)
        cc = lax.broadcasted_iota(jnp.int32, shift_ref.shape, 1)
        shift_ref[...] = jnp.where(cc == L - (CONV_WIDTH - 1) + (r % L) + r // L,
                                   1.0, 0.0).astype(BF16)
        state_ref[...] = jnp.zeros_like(state_ref)

    ext_ref[L:2 * L, 0:D_SSM] = xr_ref[...]
    ext_ref[L:2 * L, D_SSM:D_XBC] = bcr_ref[...]
    shifted = jnp.dot(shift_ref[...], ext_ref[...], preferred_element_type=F32)
    conv = cb_ref[...] + cw_ref[CONV_WIDTH - 1:CONV_WIDTH, :] * ext_ref[L:2 * L, :].astype(F32)
    for t in range(CONV_WIDTH - 1):
        conv = conv + cw_ref[t:t + 1, :] * shifted[t * L:(t + 1) * L]
    ext_ref[L - 16:L, :] = ext_ref[2 * L - 16:2 * L, :]
    xbc = _silu(conv)
    xs = xbc[:, :D_SSM]

    dt = dtr_ref[...] + dtb_ref[...]
    dt = jnp.maximum(dt, 0.0) + jnp.log1p(jnp.exp(-jnp.abs(dt)))
    a = -jnp.exp(alog_ref[...]) * dt
    row = lax.broadcasted_iota(jnp.int32, (L, L), 0)
    col = lax.broadcasted_iota(jnp.int32, (L, L), 1)
    causal = col <= row
    tril = jnp.where(causal, 1.0, 0.0).astype(BF16)
    a_cs = None
    for t in _split3(a):
        d = jnp.dot(tril, t, preferred_element_type=F32)
        a_cs = d if a_cs is None else a_cs + d
    a_cs_t = a_cs.T
    a_end = a_cs[L - 1:L, :]

    hrow = lax.broadcasted_iota(jnp.int32, (LANES, D_SSM), 0)
    ccol = lax.broadcasted_iota(jnp.int32, (LANES, D_SSM), 1)
    expand = jnp.where(ccol // SSM_HEAD_DIM == hrow, 1.0, 0.0).astype(BF16)
    dt_x = _dot3(dt, expand)
    grow_x = _dot3(jnp.exp(a_cs), expand)
    tail_x = _dot3(jnp.exp(a_end - a_cs), expand)
    end_x = grow_x[L - 1:L, :]

    xd = xs * dt_x
    xd_b = xd.astype(BF16)
    xdt_b = (xd * tail_x).astype(BF16)
    lane = lax.broadcasted_iota(jnp.int32, (L, LANES), 1)
    lo_half = lane < SSM_HEAD_DIM
    zeros_b = jnp.zeros((L, LANES), BF16)

    y_parts = []
    for g in range(SSM_GROUPS):
        b_g = xbc[:, D_SSM + g * SSM_STATE:D_SSM + (g + 1) * SSM_STATE].astype(BF16)
        c_g = xbc[:, D_SSM + (SSM_GROUPS + g) * SSM_STATE:
                  D_SSM + (SSM_GROUPS + g + 1) * SSM_STATE].astype(BF16)
        cb = lax.dot_general(c_g, b_g, (((1,), (1,)), ((), ())),
                             preferred_element_type=F32)
        for pr in range(2):
            pair = 2 * g + pr
            lanes = slice(pair * LANES, (pair + 1) * LANES)
            xd_p = xd_b[:, lanes]
            y_p = None
            for hh in range(2):
                head = 2 * pair + hh
                seg = a_cs[:, head:head + 1] - a_cs_t[head:head + 1, :]
                decay = jnp.exp(jnp.where(causal, seg, NEG_BIG))
                w = (cb * decay).astype(BF16)
                xh = jnp.where(lo_half if hh == 0 else ~lo_half, xd_p, zeros_b)
                d = jnp.dot(w, xh, preferred_element_type=F32)
                y_p = d if y_p is None else y_p + d
            st = state_ref[pair]
            y_off = jnp.dot(c_g, st.astype(BF16), preferred_element_type=F32)
            y_parts.append(y_p + y_off * grow_x[:, lanes])
            new = lax.dot_general(b_g, xdt_b[:, lanes], (((0,), (0,)), ((), ())),
                                  preferred_element_type=F32)
            state_ref[pair] = st * end_x[:, lanes] + new

    y = jnp.concatenate(y_parts, axis=-1) + xs * dskip_ref[...]
    gated = y * _silu(z_ref[...].astype(F32))
    ms = jnp.mean(gated * gated, axis=-1, keepdims=True)
    o_ref[...] = (gated * lax.rsqrt(ms + EPS) * nw_ref[...]).astype(o_ref.dtype)


def _ssd_mixer(proj, dt_raw, conv_w, conv_b, dt_bias, a_log, d_skip_x, norm_w):
    s = proj.shape[0]
    L = SSD_CHUNK
    full = lambda shape: pl.BlockSpec(shape, lambda c: (0, 0))
    return pl.pallas_call(
        _ssd_kernel,
        grid=(s // L,),
        in_specs=[
            pl.BlockSpec((L, D_SSM), lambda c: (c, 1)),
            pl.BlockSpec((L, D_SSM), lambda c: (c, 2)),
            pl.BlockSpec((L, D_SSM), lambda c: (c, 3)),
            pl.BlockSpec((L, LANES), lambda c: (c, 0)),
            full((CONV_WIDTH, D_XBC)),
            full((1, D_XBC)),
            full((1, LANES)),
            full((1, LANES)),
            full((1, D_SSM)),
            full((1, D_SSM)),
        ],
        out_specs=pl.BlockSpec((L, D_SSM), lambda c: (c, 0)),
        out_shape=jax.ShapeDtypeStruct((s, D_SSM), BF16),
        scratch_shapes=[pltpu.VMEM((2 * L, D_XBC), BF16),
                        pltpu.VMEM(((CONV_WIDTH - 1) * L, 2 * L), BF16),
                        pltpu.VMEM((N_SSM_HEADS // 2, SSM_STATE, LANES), F32)],
        compiler_params=_params("arbitrary"),
        name="ssd_mixer",
    )(proj, proj, proj, dt_raw, conv_w, conv_b, dt_bias, a_log, d_skip_x, norm_w)


def _out_proj_kernel(a_ref, b_ref, wa_ref, wb_ref, x_ref, o_ref):
    acc = jnp.dot(a_ref[...], wa_ref[...], preferred_element_type=F32)
    acc = acc + jnp.dot(b_ref[...], wb_ref[...], preferred_element_type=F32)
    o_ref[...] = x_ref[...] + acc


def _out_proj(a, b, w_stack, layer, x, tm, tn):
    s = x.shape[0]
    ka, kb = a.shape[1], b.shape[1]
    assert ka == kb
    return pl.pallas_call(
        _out_proj_kernel,
        grid=(s // tm, D_MODEL // tn),
        in_specs=[
            pl.BlockSpec((tm, ka), lambda i, j: (i, 0)),
            pl.BlockSpec((tm, kb), lambda i, j: (i, 0)),
            pl.BlockSpec((None, ka, tn), lambda i, j: (layer, 0, j)),
            pl.BlockSpec((None, kb, tn), lambda i, j: (layer, 1, j)),
            pl.BlockSpec((tm, tn), lambda i, j: (i, j)),
        ],
        out_specs=pl.BlockSpec((tm, tn), lambda i, j: (i, j)),
        out_shape=jax.ShapeDtypeStruct((s, D_MODEL), F32),
        compiler_params=_params("parallel", "arbitrary"),
        name="out_proj",
    )(a, b, w_stack, w_stack, x)


def _ffn_kernel(x_ref, nw_ref, wg_ref, wu_ref, wd_ref, fw_ref, o_ref, h_ref, *, final_norm):
    f = pl.program_id(1)

    @pl.when(f == 0)
    def _():
        xf = x_ref[...]
        ms = jnp.mean(xf * xf, axis=-1, keepdims=True)
        h_ref[...] = (xf * lax.rsqrt(ms + EPS) * nw_ref[...]).astype(BF16)
        o_ref[...] = xf

    h = h_ref[...]
    g = jnp.dot(h, wg_ref[...], preferred_element_type=F32)
    u = jnp.dot(h, wu_ref[...], preferred_element_type=F32)
    act = (_silu(g) * u).astype(BF16)
    o_ref[...] += jnp.dot(act, wd_ref[...], preferred_element_type=F32)

    if final_norm:
        @pl.when(f == pl.num_programs(1) - 1)
        def _():
            xf = o_ref[...]
            ms = jnp.mean(xf * xf, axis=-1, keepdims=True)
            o_ref[...] = xf * lax.rsqrt(ms + EPS) * fw_ref[...]


def _ffn(x, nw, w_gu_stack, w_d_stack, layer, final_w, final_norm, tm, tf):
    s, d = x.shape
    nf = D_FF // tf
    kern = functools.partial(_ffn_kernel, final_norm=final_norm)
    return pl.pallas_call(
        kern,
        grid=(s // tm, nf),
        in_specs=[
            pl.BlockSpec((tm, d), lambda i, f: (i, 0)),
            pl.BlockSpec((1, d), lambda i, f: (0, 0)),
            pl.BlockSpec((None, d, tf), lambda i, f: (layer, 0, f)),
            pl.BlockSpec((None, d, tf), lambda i, f: (layer, 0, nf + f)),
            pl.BlockSpec((None, tf, d), lambda i, f: (layer, f, 0)),
            pl.BlockSpec((1, d), lambda i, f: (0, 0)),
        ],
        out_specs=pl.BlockSpec((tm, d), lambda i, f: (i, 0)),
        out_shape=jax.ShapeDtypeStruct((s, d), F32),
        scratch_shapes=[pltpu.VMEM((tm, d), BF16)],
        compiler_params=_params("parallel", "arbitrary"),
        name="ffn",
    )(x, nw, w_gu_stack, w_gu_stack, w_d_stack, final_w)


def _alibi_slopes(n_heads):
    start = 2.0 ** (-8.0 / n_heads)
    return start ** jnp.arange(1, n_heads + 1, dtype=F32)


def kernel(x, norm_mix_w, w_in, diff_lambda, subln_w, conv_w, conv_b, dt_bias, a_log,
           d_skip, ssm_norm_w, w_out, norm_ffn_w, w_gate_up, w_down, norm_final_w):
    b, s_len, _ = x.shape
    assert b == 1
    depth = w_in.shape[0]
    o_k, o_v, o_z = D_ATTN, 2 * D_ATTN, 3 * D_ATTN
    o_dt = o_z + D_SSM + D_XBC
    n_dt = w_in.shape[2] - o_dt
    tm = min(512, s_len)
    tm_big = min(1024, s_len)
    tq = min(512, s_len)
    hps = ATTN_HEADS_PER_STEP

    slopes = _alibi_slopes(N_ATTN_HEADS)
    q_scale = ATTN_HEAD_DIM ** -0.5 * LOG2E
    colscale = jnp.concatenate([jnp.full((1, D_ATTN), q_scale, F32),
                                jnp.ones((1, o_dt - o_z), F32)], axis=1)
    pad_l = lambda v: jnp.pad(v.astype(F32), (0, LANES - v.shape[0]))[None, :]
    w_in_b = w_in.astype(BF16)
    w_out_b = w_out.astype(BF16)
    w_gu_b = w_gate_up.astype(BF16)
    w_d_b = w_down.astype(BF16)
    tn = D_ATTN
    main_blocks = (0,) + tuple(range(o_z // tn, o_dt // tn))

    xc = x[0]
    for l in range(depth):
        w_vt = w_in_b[l, :, o_v:o_z].T
        w_dt = jnp.pad(w_in_b[l, :, o_dt:], ((0, 0), (0, LANES - n_dt)))
        nw = norm_mix_w[l][None, :]
        proj = _rms_matmul(xc, nw, w_in_b, l, main_blocks, colscale, BF16, tm_big, tn)
        k_groups, vt, dt_raw = _kv_dt_proj(xc, nw, w_in_b, l, o_k // tn, tn, w_vt, w_dt,
                                           hps * LANES, tm)

        lam_init = 0.8 - 0.6 * math.exp(-0.3 * l)
        attn_out = _diff_attention(proj, k_groups, vt, slopes, diff_lambda[l], subln_w[l][:, None],
                                   lam_init, tq, hps)
        ssm_out = _ssd_mixer(proj, dt_raw, conv_w[l], conv_b[l][None, :], pad_l(dt_bias[l]),
                             pad_l(a_log[l]), jnp.repeat(d_skip[l], SSM_HEAD_DIM)[None, :],
                             ssm_norm_w[l][None, :])
        xc = _out_proj(attn_out, ssm_out, w_out_b, l, xc, tm, D_MODEL)
        xc = _ffn(xc, norm_ffn_w[l][None, :], w_gu_b, w_d_b, l, norm_final_w[None, :],
                  l == depth - 1, tm_big, 512)
    return xc[None]
```

```python
import functools
import math

import jax
import jax.numpy as jnp
from jax import lax
from jax.experimental import pallas as pl
from jax.experimental.pallas import tpu as pltpu

D_MODEL = 2048
CHUNK = 64
D_ATTN = D_MODEL // 2
D_SSM = D_MODEL - D_ATTN
ATTN_HEAD_DIM = 64
ATTN_VDIM = 2 * ATTN_HEAD_DIM
N_ATTN_HEADS = D_ATTN // ATTN_VDIM
SSM_HEAD_DIM = 64
N_SSM_HEADS = D_SSM // SSM_HEAD_DIM
SSM_STATE = 128
SSM_GROUPS = 4
CONV_WIDTH = 4
SSD_CHUNK = 128
D_XBC = D_SSM + 2 * SSM_GROUPS * SSM_STATE
D_FF = ((8 * D_MODEL // 3 + 255) // 256) * 256
EPS = 1e-5

LANES = 128
LOG2E = math.log2(math.e)
NEG_BIG = -1e30
VMEM_LIMIT = 56 * 1024 * 1024
ATTN_HEADS_PER_STEP = 2

F32 = jnp.float32
BF16 = jnp.bfloat16


def _params(*sem):
    return pltpu.CompilerParams(dimension_semantics=sem, vmem_limit_bytes=VMEM_LIMIT)


def _rms_matmul_kernel(x_ref, nw_ref, w_ref, cs_ref, o_ref, h_ref):
    @pl.when(pl.program_id(1) == 0)
    def _():
        xf = x_ref[...]
        ms = jnp.mean(xf * xf, axis=-1, keepdims=True)
        h_ref[...] = (xf * lax.rsqrt(ms + EPS) * nw_ref[...]).astype(BF16)

    acc = jnp.dot(h_ref[...], w_ref[...], preferred_element_type=F32)
    o_ref[...] = (acc * cs_ref[...]).astype(o_ref.dtype)


def _rms_matmul(x, nw, w_stack, layer, col_blocks, colscale, out_dtype, tm, tn):
    s, d = x.shape
    n = len(col_blocks) * tn
    first, gap = col_blocks[0], col_blocks[1] - col_blocks[0] - 1
    assert tuple(col_blocks) == (first,) + tuple(first + gap + j for j in range(1, len(col_blocks)))
    return pl.pallas_call(
        _rms_matmul_kernel,
        grid=(s // tm, n // tn),
        in_specs=[
            pl.BlockSpec((tm, d), lambda i, j: (i, 0)),
            pl.BlockSpec((1, d), lambda i, j: (0, 0)),
            pl.BlockSpec((None, d, tn),
                         lambda i, j: (layer, 0, first + j + gap * jnp.minimum(j, 1))),
            pl.BlockSpec((1, tn), lambda i, j: (0, j)),
        ],
        out_specs=pl.BlockSpec((tm, tn), lambda i, j: (i, j)),
        out_shape=jax.ShapeDtypeStruct((s, n), out_dtype),
        scratch_shapes=[pltpu.VMEM((tm, d), BF16)],
        compiler_params=_params("parallel", "arbitrary"),
        name="rms_matmul",
    )(x, nw, w_stack, colscale)


def _kv_dt_kernel(x_ref, nw_ref, wk_ref, wvt_ref, wdt_ref, k_ref, vt_ref, dt_ref):
    xf = x_ref[...]
    ms = jnp.mean(xf * xf, axis=-1, keepdims=True)
    h = (xf * lax.rsqrt(ms + EPS) * nw_ref[...]).astype(BF16)
    k = jnp.dot(h, wk_ref[...], preferred_element_type=F32)
    gw = k_ref.shape[2]
    for grp in range(k_ref.shape[0]):
        k_ref[grp] = k[:, grp * gw:(grp + 1) * gw].astype(k_ref.dtype)
    vt_ref[...] = lax.dot_general(wvt_ref[...], h, (((1,), (1,)), ((), ())),
                                  preferred_element_type=F32).astype(vt_ref.dtype)
    dt_ref[...] = jnp.dot(h, wdt_ref[...], preferred_element_type=F32)


def _kv_dt_proj(x, nw, w_stack, layer, k_block, nk, wvt, wdt, gw, tm):
    s, d = x.shape
    nv, ndt = wvt.shape[0], wdt.shape[1]
    full = lambda shape: pl.BlockSpec(shape, lambda i: (0, 0))
    return pl.pallas_call(
        _kv_dt_kernel,
        grid=(s // tm,),
        in_specs=[
            pl.BlockSpec((tm, d), lambda i: (i, 0)),
            full((1, d)),
            pl.BlockSpec((None, d, nk), lambda i: (layer, 0, k_block)),
            full((nv, d)),
            full((d, ndt)),
        ],
        out_specs=[
            pl.BlockSpec((nk // gw, tm, gw), lambda i: (0, i, 0)),
            pl.BlockSpec((nv, tm), lambda i: (0, i)),
            pl.BlockSpec((tm, ndt), lambda i: (i, 0)),
        ],
        out_shape=[
            jax.ShapeDtypeStruct((nk // gw, s, gw), BF16),
            jax.ShapeDtypeStruct((nv, s), BF16),
            jax.ShapeDtypeStruct((s, ndt), F32),
        ],
        compiler_params=_params("parallel"),
        name="kv_dt_proj",
    )(x, nw, w_stack, wvt, wdt)


ONES_ROWS = 16
ACC_ROWS = ATTN_VDIM + ONES_ROWS


def _split3(a):
    hi = a.astype(BF16)
    r1 = a - hi.astype(F32)
    mid = r1.astype(BF16)
    lo = (r1 - mid.astype(F32)).astype(BF16)
    return hi, mid, lo


def _attn_kernel(slopes_ref, dl_ref, q_ref, k_ref, vt_ref, sw_ref, o_ref,
                 acc_ref, s_ref, p_ref, *, tq, hps, lam_init):
    g = pl.program_id(0)
    i = pl.program_id(1)
    heads = range(hps)
    hm = [(hh, j) for hh in heads for j in range(2)]
    cs = [slopes_ref[g * hps + hh] * LOG2E for hh in heads]

    shape = (tq, LANES)
    lane = lax.broadcasted_iota(jnp.int32, shape, 1)
    zero = jnp.zeros(shape, BF16)
    krow = lax.broadcasted_iota(jnp.int32, shape, 0)
    k_split = jnp.where((lane & 1) == 0, (krow >> 4) << 4, krow & 15)
    k_bias = jnp.where(lane < 6, k_split, 0).astype(F32).astype(BF16)
    q_maps = {}
    for hh in heads:
        q = q_ref[:, hh * LANES:(hh + 1) * LANES]
        c_terms = _split3(jnp.full(shape, cs[hh], F32))
        q_bias = jnp.zeros(shape, F32)
        for t in range(3):
            q_bias = jnp.where((lane >> 1) == t, c_terms[t].astype(F32), q_bias)
        q_bias = q_bias.astype(BF16)
        q_maps[hh, 0] = jnp.concatenate([jnp.where(lane < ATTN_HEAD_DIM, q, zero), q_bias], axis=1)
        q_maps[hh, 1] = jnp.concatenate([jnp.where(lane >= ATTN_HEAD_DIM, q, zero), q_bias], axis=1)

    acc_ref[...] = jnp.zeros_like(acc_ref)
    ones_rows = jnp.ones((ONES_ROWS, tq), BF16)

    def scores(blk, slot):
        k0 = pl.multiple_of(blk * tq, tq)
        col_max = {}
        for hh in heads:
            kblk = jnp.concatenate([k_ref[pl.ds(k0, tq), hh * LANES:(hh + 1) * LANES], k_bias],
                                   axis=1)
            for j in range(2):
                s = lax.dot_general(kblk, q_maps[hh, j], (((1,), (1,)), ((), ())),
                                    preferred_element_type=F32)
                s_ref[slot, hh, j] = s
                col_max[hh, j] = jnp.max(s, axis=0, keepdims=True)
        return col_max

    def softmax(slot, bias, offs, ms, col_max):
        new_ms, alphas = {}, {}
        half, rows = tq // 2, tq // 8
        for hh, j in hm:
            if bias is None:
                m_new = jnp.maximum(ms[hh, j], col_max[hh, j] + offs[hh])
                shift = m_new - offs[hh]
                for c0 in (0, half):
                    for r0 in range(0, tq, rows):
                        p_ref[slot, hh, j, r0:r0 + rows, c0:c0 + half] = jnp.exp2(
                            s_ref[slot, hh, j, r0:r0 + rows, c0:c0 + half]
                            - shift[:, c0:c0 + half]).astype(BF16)
            else:
                s = s_ref[slot, hh, j] + bias[hh]
                m_new = jnp.maximum(ms[hh, j], jnp.max(s, axis=0, keepdims=True) + offs[hh])
                p_ref[slot, hh, j] = jnp.exp2(s - (m_new - offs[hh])).astype(BF16)
            alphas[hh, j] = jnp.exp2(ms[hh, j] - m_new)
            new_ms[hh, j] = m_new
        return new_ms, alphas

    def values(blk, slot, alphas):
        k0 = pl.multiple_of(blk * tq, tq)
        for hh in heads:
            lhs = jnp.concatenate([vt_ref[hh * ATTN_VDIM:(hh + 1) * ATTN_VDIM, pl.ds(k0, tq)],
                                   ones_rows], axis=0)
            for j in range(2):
                pv = jnp.dot(lhs, p_ref[slot, hh, j], preferred_element_type=F32)
                acc_ref[hh, j] = alphas[hh, j] * acc_ref[hh, j] + pv

    init_m = {k: jnp.full((1, tq), NEG_BIG, F32) for k in hm}
    init_a = {k: jnp.ones((1, tq), F32) for k in hm}
    col_max0 = scores(0, 0)

    def past_block(t, slot, has_prev, carry):
        ms, prev_alphas, col_max = carry
        dist = ((i - t) * tq).astype(F32)
        ms, alphas = softmax(slot, None, [-cs[hh] * dist for hh in heads], ms, col_max)
        col_max = scores(t + 1, 1 - slot)
        if has_prev:
            values(t - 1, 1 - slot, prev_alphas)
        return ms, alphas, col_max

    def later_step(t, carry):
        return lax.cond((t & 1) == 0, functools.partial(past_block, t, 0, True),
                        functools.partial(past_block, t, 1, True), carry)

    def past_step(t, carry):
        return lax.cond(t == 0, functools.partial(past_block, t, 0, False),
                        functools.partial(later_step, t), carry)

    carry = lax.fori_loop(0, i, past_step, (init_m, init_a, col_max0))

    dl = dl_ref[...]
    lam = (jnp.exp(jnp.sum(dl[0:1] * dl[1:2], axis=-1, keepdims=True))
           - jnp.exp(jnp.sum(dl[2:3] * dl[3:4], axis=-1, keepdims=True)) + lam_init)

    def finish(slot, has_prev):
        ms, alphas, _ = carry
        if has_prev:
            values(i - 1, 1 - slot, alphas)
        kl = lax.broadcasted_iota(jnp.int32, (tq, tq), 0)
        ql = lax.broadcasted_iota(jnp.int32, (tq, tq), 1)
        allowed = (kl // CHUNK) <= (ql // CHUNK)
        ahead = jnp.maximum(kl - ql, 0).astype(F32)
        bias = [jnp.where(allowed, (-2.0 * cs[hh]) * ahead, NEG_BIG) for hh in heads]
        ms, alphas = softmax(slot, bias, [0.0] * hps, ms, None)
        values(i, slot, alphas)
        for hh in heads:
            a1, a2 = acc_ref[hh, 0], acc_ref[hh, 1]
            o = (a1[:ATTN_VDIM] * (1.0 / a1[ATTN_VDIM:ATTN_VDIM + 1])
                 - lam * (a2[:ATTN_VDIM] * (1.0 / a2[ATTN_VDIM:ATTN_VDIM + 1])))
            ms_o = jnp.mean(o * o, axis=0, keepdims=True)
            y = (o * lax.rsqrt(ms_o + EPS) * sw_ref[...]) * (1.0 - lam_init)
            o_ref[:, hh * LANES:(hh + 1) * LANES] = y.T.astype(o_ref.dtype)

    pl.when(i == 0)(functools.partial(finish, 0, False))
    pl.when((i > 0) & ((i & 1) == 0))(functools.partial(finish, 0, True))
    pl.when((i & 1) == 1)(functools.partial(finish, 1, True))


def _diff_attention(proj, k_groups, vt, slopes, dl, subw_col, lam_init, tq, hps):
    s = proj.shape[0]
    w = hps * LANES
    kern = functools.partial(_attn_kernel, tq=tq, hps=hps, lam_init=lam_init)
    return pl.pallas_call(
        kern,
        grid=(N_ATTN_HEADS // hps, s // tq),
        in_specs=[
            pl.BlockSpec(memory_space=pltpu.SMEM),
            pl.BlockSpec((4, ATTN_HEAD_DIM), lambda g, i: (0, 0)),
            pl.BlockSpec((tq, w), lambda g, i: (i, g)),
            pl.BlockSpec((None, s, w), lambda g, i: (g, 0, 0)),
            pl.BlockSpec((hps * ATTN_VDIM, s), lambda g, i: (g, 0)),
            pl.BlockSpec((ATTN_VDIM, 1), lambda g, i: (0, 0)),
        ],
        out_specs=pl.BlockSpec((tq, w), lambda g, i: (i, g)),
        out_shape=jax.ShapeDtypeStruct((s, D_ATTN), BF16),
        scratch_shapes=[pltpu.VMEM((hps, 2, ACC_ROWS, tq), F32),
                        pltpu.VMEM((2, hps, 2, tq, tq), F32),
                        pltpu.VMEM((2, hps, 2, tq, tq), BF16)],
        compiler_params=_params("parallel", "arbitrary"),
        name="diff_attention",
    )(slopes, dl, proj, k_groups, vt, subw_col)


def _dot3(a, b_bf16):
    out = None
    for t in _split3(a):
        d = jnp.dot(t, b_bf16, preferred_element_type=F32)
        out = d if out is None else out + d
    return out


def _silu(x):
    half = 0.5 * x
    return half + half * jnp.tanh(half)


def _ssd_kernel(z_ref, xr_ref, bcr_ref, dtr_ref, cw_ref, cb_ref, dtb_ref, alog_ref,
                dskip_ref, nw_ref, o_ref, ext_ref, shift_ref, state_ref):
    L = SSD_CHUNK
    c = pl.program_id(0)

    @pl.when(c == 0)
    def _():
        ext_ref[0:L, :] = jnp.zeros((L, D_XBC), BF16)
        r = lax.broadcasted_iota(jnp.int32, shift_ref.shape, 0)
        cc = lax.broadcasted_iota(jnp.int32, shift_ref.shape, 1)
        shift_ref[...] = jnp.where(cc == L - (CONV_WIDTH - 1) + (r % L) + r // L,
                                   1.0, 0.0).astype(BF16)
        state_ref[...] = jnp.zeros_like(state_ref)

    ext_ref[L:2 * L, 0:D_SSM] = xr_ref[...]
    ext_ref[L:2 * L, D_SSM:D_XBC] = bcr_ref[...]
    shifted = jnp.dot(shift_ref[...], ext_ref[...], preferred_element_type=F32)
    conv = cb_ref[...] + cw_ref[CONV_WIDTH - 1:CONV_WIDTH, :] * ext_ref[L:2 * L, :].astype(F32)
    for t in range(CONV_WIDTH - 1):
        conv = conv + cw_ref[t:t + 1, :] * shifted[t * L:(t + 1) * L]
    ext_ref[L - 16:L, :] = ext_ref[2 * L - 16:2 * L, :]
    xbc = _silu(conv)
    xs = xbc[:, :D_SSM]

    dt = dtr_ref[...] + dtb_ref[...]
    dt = jnp.maximum(dt, 0.0) + jnp.log1p(jnp.exp(-jnp.abs(dt)))
    a = -jnp.exp(alog_ref[...]) * dt
    row = lax.broadcasted_iota(jnp.int32, (L, L), 0)
    col = lax.broadcasted_iota(jnp.int32, (L, L), 1)
    causal = col <= row
    tril = jnp.where(causal, 1.0, 0.0).astype(BF16)
    a_cs = None
    for t in _split3(a):
        d = jnp.dot(tril, t, preferred_element_type=F32)
        a_cs = d if a_cs is None else a_cs + d
    a_cs_t = a_cs.T
    a_end = a_cs[L - 1:L, :]

    hrow = lax.broadcasted_iota(jnp.int32, (LANES, D_SSM), 0)
    ccol = lax.broadcasted_iota(jnp.int32, (LANES, D_SSM), 1)
    expand = jnp.where(ccol // SSM_HEAD_DIM == hrow, 1.0, 0.0).astype(BF16)
    dt_x = _dot3(dt, expand)
    grow_x = _dot3(jnp.exp(a_cs), expand)
    tail_x = _dot3(jnp.exp(a_end - a_cs), expand)
    end_x = grow_x[L - 1:L, :]

    xd = xs * dt_x
    xd_b = xd.astype(BF16)
    xdt_b = (xd * tail_x).astype(BF16)
    lane = lax.broadcasted_iota(jnp.int32, (L, LANES), 1)
    lo_half = lane < SSM_HEAD_DIM
    zeros_b = jnp.zeros((L, LANES), BF16)

    y_parts = []
    for g in range(SSM_GROUPS):
        b_g = xbc[:, D_SSM + g * SSM_STATE:D_SSM + (g + 1) * SSM_STATE].astype(BF16)
        c_g = xbc[:, D_SSM + (SSM_GROUPS + g) * SSM_STATE:
                  D_SSM + (SSM_GROUPS + g + 1) * SSM_STATE].astype(BF16)
        cb = lax.dot_general(c_g, b_g, (((1,), (1,)), ((), ())),
                             preferred_element_type=F32)
        for pr in range(2):
            pair = 2 * g + pr
            lanes = slice(pair * LANES, (pair + 1) * LANES)
            xd_p = xd_b[:, lanes]
            y_p = None
            for hh in range(2):
                head = 2 * pair + hh
                seg = a_cs[:, head:head + 1] - a_cs_t[head:head + 1, :]
                decay = jnp.exp(jnp.where(causal, seg, NEG_BIG))
                w = (cb * decay).astype(BF16)
                xh = jnp.where(lo_half if hh == 0 else ~lo_half, xd_p, zeros_b)
                d = jnp.dot(w, xh, preferred_element_type=F32)
                y_p = d if y_p is None else y_p + d
            st = state_ref[pair]
            y_off = jnp.dot(c_g, st.astype(BF16), preferred_element_type=F32)
            y_parts.append(y_p + y_off * grow_x[:, lanes])
            new = lax.dot_general(b_g, xdt_b[:, lanes], (((0,), (0,)), ((), ())),
                                  preferred_element_type=F32)
            state_ref[pair] = st * end_x[:, lanes] + new

    y = jnp.concatenate(y_parts, axis=-1) + xs * dskip_ref[...]
    gated = y * _silu(z_ref[...].astype(F32))
    ms = jnp.mean(gated * gated, axis=-1, keepdims=True)
    o_ref[...] = (gated * lax.rsqrt(ms + EPS) * nw_ref[...]).astype(o_ref.dtype)


def _ssd_mixer(proj, dt_raw, conv_w, conv_b, dt_bias, a_log, d_skip_x, norm_w):
    s = proj.shape[0]
    L = SSD_CHUNK
    full = lambda shape: pl.BlockSpec(shape, lambda c: (0, 0))
    return pl.pallas_call(
        _ssd_kernel,
        grid=(s // L,),
        in_specs=[
            pl.BlockSpec((L, D_SSM), lambda c: (c, 1)),
            pl.BlockSpec((L, D_SSM), lambda c: (c, 2)),
            pl.BlockSpec((L, D_SSM), lambda c: (c, 3)),
            pl.BlockSpec((L, LANES), lambda c: (c, 0)),
            full((CONV_WIDTH, D_XBC)),
            full((1, D_XBC)),
            full((1, LANES)),
            full((1, LANES)),
            full((1, D_SSM)),
            full((1, D_SSM)),
        ],
        out_specs=pl.BlockSpec((L, D_SSM), lambda c: (c, 0)),
        out_shape=jax.ShapeDtypeStruct((s, D_SSM), BF16),
        scratch_shapes=[pltpu.VMEM((2 * L, D_XBC), BF16),
                        pltpu.VMEM(((CONV_WIDTH - 1) * L, 2 * L), BF16),
                        pltpu.VMEM((N_SSM_HEADS // 2, SSM_STATE, LANES), F32)],
        compiler_params=_params("arbitrary"),
        name="ssd_mixer",
    )(proj, proj, proj, dt_raw, conv_w, conv_b, dt_bias, a_log, d_skip_x, norm_w)


def _out_proj_kernel(a_ref, b_ref, wa_ref, wb_ref, x_ref, o_ref):
    acc = jnp.dot(a_ref[...], wa_ref[...], preferred_element_type=F32)
    acc = acc + jnp.dot(b_ref[...], wb_ref[...], preferred_element_type=F32)
    o_ref[...] = x_ref[...] + acc


def _out_proj(a, b, w_stack, layer, x, tm, tn):
    s = x.shape[0]
    ka, kb = a.shape[1], b.shape[1]
    assert ka == kb
    return pl.pallas_call(
        _out_proj_kernel,
        grid=(s // tm, D_MODEL // tn),
        in_specs=[
            pl.BlockSpec((tm, ka), lambda i, j: (i, 0)),
            pl.BlockSpec((tm, kb), lambda i, j: (i, 0)),
            pl.BlockSpec((None, ka, tn), lambda i, j: (layer, 0, j)),
            pl.BlockSpec((None, kb, tn), lambda i, j: (layer, 1, j)),
            pl.BlockSpec((tm, tn), lambda i, j: (i, j)),
        ],
        out_specs=pl.BlockSpec((tm, tn), lambda i, j: (i, j)),
        out_shape=jax.ShapeDtypeStruct((s, D_MODEL), F32),
        compiler_params=_params("parallel", "arbitrary"),
        name="out_proj",
    )(a, b, w_stack, w_stack, x)


def _ffn_kernel(x_ref, nw_ref, wg_ref, wu_ref, wd_ref, fw_ref, o_ref, h_ref, *, final_norm):
    f = pl.program_id(1)

    @pl.when(f == 0)
    def _():
        xf = x_ref[...]
        ms = jnp.mean(xf * xf, axis=-1, keepdims=True)
        h_ref[...] = (xf * lax.rsqrt(ms + EPS) * nw_ref[...]).astype(BF16)
        o_ref[...] = xf

    h = h_ref[...]
    g = jnp.dot(h, wg_ref[...], preferred_element_type=F32)
    u = jnp.dot(h, wu_ref[...], preferred_element_type=F32)
    act = (_silu(g) * u).astype(BF16)
    o_ref[...] += jnp.dot(act, wd_ref[...], preferred_element_type=F32)

    if final_norm:
        @pl.when(f == pl.num_programs(1) - 1)
        def _():
            xf = o_ref[...]
            ms = jnp.mean(xf * xf, axis=-1, keepdims=True)
            o_ref[...] = xf * lax.rsqrt(ms + EPS) * fw_ref[...]


def _ffn(x, nw, w_gu_stack, w_d_stack, layer, final_w, final_norm, tm, tf):
    s, d = x.shape
    nf = D_FF // tf
    kern = functools.partial(_ffn_kernel, final_norm=final_norm)
    return pl.pallas_call(
        kern,
        grid=(s // tm, nf),
        in_specs=[
            pl.BlockSpec((tm, d), lambda i, f: (i, 0)),
            pl.BlockSpec((1, d), lambda i, f: (0, 0)),
            pl.BlockSpec((None, d, tf), lambda i, f: (layer, 0, f)),
            pl.BlockSpec((None, d, tf), lambda i, f: (layer, 0, nf + f)),
            pl.BlockSpec((None, tf, d), lambda i, f: (layer, f, 0)),
            pl.BlockSpec((1, d), lambda i, f: (0, 0)),
        ],
        out_specs=pl.BlockSpec((tm, d), lambda i, f: (i, 0)),
        out_shape=jax.ShapeDtypeStruct((s, d), F32),
        scratch_shapes=[pltpu.VMEM((tm, d), BF16)],
        compiler_params=_params("parallel", "arbitrary"),
        name="ffn",
    )(x, nw, w_gu_stack, w_gu_stack, w_d_stack, final_w)


def _alibi_slopes(n_heads):
    start = 2.0 ** (-8.0 / n_heads)
    return start ** jnp.arange(1, n_heads + 1, dtype=F32)


def kernel(x, norm_mix_w, w_in, diff_lambda, subln_w, conv_w, conv_b, dt_bias, a_log,
           d_skip, ssm_norm_w, w_out, norm_ffn_w, w_gate_up, w_down, norm_final_w):
    b, s_len, _ = x.shape
    assert b == 1
    depth = w_in.shape[0]
    o_k, o_v, o_z = D_ATTN, 2 * D_ATTN, 3 * D_ATTN
    o_dt = o_z + D_SSM + D_XBC
    n_dt = w_in.shape[2] - o_dt
    tm = min(512, s_len)
    tm_big = min(1024, s_len)
    tq = min(512, s_len)
    hps = ATTN_HEADS_PER_STEP

    slopes = _alibi_slopes(N_ATTN_HEADS)
    q_scale = ATTN_HEAD_DIM ** -0.5 * LOG2E
    colscale = jnp.concatenate([jnp.full((1, D_ATTN), q_scale, F32),
                                jnp.ones((1, o_dt - o_z), F32)], axis=1)
    pad_l = lambda v: jnp.pad(v.astype(F32), (0, LANES - v.shape[0]))[None, :]
    w_in_b = w_in.astype(BF16)
    w_out_b = w_out.astype(BF16)
    w_gu_b = w_gate_up.astype(BF16)
    w_d_b = w_down.astype(BF16)
    tn = D_ATTN
    main_blocks = (0,) + tuple(range(o_z // tn, o_dt // tn))

    xc = x[0]
    for l in range(depth):
        w_vt = w_in_b[l, :, o_v:o_z].T
        w_dt = jnp.pad(w_in_b[l, :, o_dt:], ((0, 0), (0, LANES - n_dt)))
        nw = norm_mix_w[l][None, :]
        proj = _rms_matmul(xc, nw, w_in_b, l, main_blocks, colscale, BF16, tm_big, tn)
        k_groups, vt, dt_raw = _kv_dt_proj(xc, nw, w_in_b, l, o_k // tn, tn, w_vt, w_dt,
                                           hps * LANES, tm)

        lam_init = 0.8 - 0.6 * math.exp(-0.3 * l)
        attn_out = _diff_attention(proj, k_groups, vt, slopes, diff_lambda[l], subln_w[l][:, None],
                                   lam_init, tq, hps)
        ssm_out = _ssd_mixer(proj, dt_raw, conv_w[l], conv_b[l][None, :], pad_l(dt_bias[l]),
                             pad_l(a_log[l]), jnp.repeat(d_skip[l], SSM_HEAD_DIM)[None, :],
                             ssm_norm_w[l][None, :])
        xc = _out_proj(attn_out, ssm_out, w_out_b, l, xc, tm, D_MODEL)
        xc = _ffn(xc, norm_ffn_w[l][None, :], w_gu_b, w_d_b, l, norm_final_w[None, :],
                  l == depth - 1, tm_big, 512)
    return xc[None]
```
